```python
import math
import jax, jax.numpy as jnp
from jax import lax
import numpy as np

D_MODEL = 1024
BATCH = 2
SEQ = 8192
DEPTH = 2

GRID_W = 64
CTX_LEN = 256
ROPE_THETA = 10000.0
EPS = 1e-6
Q_BLOCK = 128

MLA_HEADS = 4
MLA_Q_RANK = 256
MLA_KV_RANK = 128
MLA_NOPE = 64
MLA_ROPE = 32
MLA_V = 64
DIFF_HEADS = 4
DIFF_QK = 32
DIFF_V = 2 * DIFF_QK
GQA_HEADS = 8
GQA_KV_HEADS = 2
GQA_DIM = 64
GQA_GROUP = GQA_HEADS // GQA_KV_HEADS

MIX_WIDTH = MLA_HEADS * MLA_V + DIFF_HEADS * DIFF_V + GQA_HEADS * GQA_DIM
FFN_HIDDEN = -(-8 * D_MODEL // (3 * 256)) * 256

IN_SIZES = (
    MLA_Q_RANK,
    MLA_KV_RANK + MLA_ROPE,
    DIFF_HEADS * 2 * DIFF_QK,
    DIFF_HEADS * 2 * DIFF_QK,
    DIFF_HEADS * DIFF_V,
    GQA_HEADS * GQA_DIM,
    GQA_KV_HEADS * GQA_DIM,
    GQA_KV_HEADS * GQA_DIM,
)
IN_WIDTH = sum(IN_SIZES)

MLA_SCALE = 1.0 / math.sqrt(MLA_NOPE + MLA_ROPE)
DIFF_SCALE = 1.0 / math.sqrt(DIFF_QK)
GQA_SCALE = 1.0 / math.sqrt(GQA_DIM)

kernel_name = "hymba_style_mla_diff_gqa_dit_block"


def _rms(x, g):
    xf = x.astype(jnp.float32)
    y = xf * lax.rsqrt(jnp.mean(xf * xf, axis=-1, keepdims=True) + EPS)
    return (y * g.astype(jnp.float32)).astype(x.dtype)


def _axial_rope_tables(row, col, rot_dim):
    quarter = rot_dim // 4
    inv = ROPE_THETA ** (-jnp.arange(quarter, dtype=jnp.float32) / quarter)
    ang = jnp.concatenate([row.astype(jnp.float32)[:, None] * inv,
                           col.astype(jnp.float32)[:, None] * inv], axis=-1)
    return jnp.cos(ang), jnp.sin(ang)


def _rope(x, tables):
    if tables is None:
        return x
    cos, sin = tables
    half = x.shape[-1] // 2
    xf = x.astype(jnp.float32)
    x1, x2 = xf[..., :half], xf[..., half:]
    c = cos[None, :, None, :]
    s = sin[None, :, None, :]
    return jnp.concatenate([x1 * c - x2 * s, x2 * c + x1 * s], axis=-1).astype(x.dtype)


def _hf(t):
    return t.transpose(0, 2, 1, 3)


def _sdpa(q, k, v, scale):
    s = jnp.einsum('bhgqd,bhkd->bhgqk', q, k).astype(jnp.float32) * scale
    p = jax.nn.softmax(s, axis=-1).astype(v.dtype)
    return jnp.einsum('bhgqk,bhkd->bhgqd', p, v)


def _latent_attention(q, k, v, scale):
    B, Hk, G, S, d = q.shape
    nb = S // Q_BLOCK
    qb = q.reshape(B, Hk, G, nb, Q_BLOCK, d).transpose(3, 0, 1, 2, 4, 5)
    ob = lax.map(lambda qi: _sdpa(qi, k, v, scale), qb)
    return ob.transpose(1, 2, 3, 0, 4, 5).reshape(B, Hk, G, S, ob.shape[-1])


def _project(h, w_in, g_mla_q, w_mla_qb, g_mla_kv, w_mla_kvb, g_gqa_q, g_gqa_k,
             rope_small, rope_large):
    B, T, _ = h.shape
    p = h @ w_in
    idx = np.cumsum(IN_SIZES)[:-1].tolist()
    q_a, kv_a, dq, dk, dv, gq, gk, gv = jnp.split(p, idx, axis=-1)

    q = (_rms(q_a, g_mla_q) @ w_mla_qb).reshape(B, T, MLA_HEADS, MLA_NOPE + MLA_ROPE)
    q_nope, q_pe = q[..., :MLA_NOPE], _rope(q[..., MLA_NOPE:], rope_small)
    c_kv = kv_a[..., :MLA_KV_RANK]
    k_pe = _rope(kv_a[..., MLA_KV_RANK:][:, :, None, :], rope_small)
    kv = (_rms(c_kv, g_mla_kv) @ w_mla_kvb).reshape(B, T, MLA_HEADS, MLA_NOPE + MLA_V)
    k_nope, v_mla = kv[..., :MLA_NOPE], kv[..., MLA_NOPE:]
    q_mla = jnp.concatenate([q_nope, q_pe], axis=-1)
    k_mla = jnp.concatenate([k_nope, jnp.broadcast_to(k_pe, (B, T, MLA_HEADS, MLA_ROPE))], axis=-1)
    mla = (_hf(q_mla)[:, :, None], _hf(k_mla), _hf(v_mla))

    q_d = _rope(dq.reshape(B, T, DIFF_HEADS * 2, DIFF_QK), rope_small)
    k_d = _rope(dk.reshape(B, T, DIFF_HEADS * 2, DIFF_QK), rope_small)
    v_d = jnp.repeat(dv.reshape(B, T, DIFF_HEADS, DIFF_V), 2, axis=2)
    diff = (_hf(q_d)[:, :, None], _hf(k_d), _hf(v_d))

    q_g = _rope(_rms(gq.reshape(B, T, GQA_HEADS, GQA_DIM), g_gqa_q), rope_large)
    k_g = _rope(_rms(gk.reshape(B, T, GQA_KV_HEADS, GQA_DIM), g_gqa_k), rope_large)
    v_g = gv.reshape(B, T, GQA_KV_HEADS, GQA_DIM)
    q_g = q_g.reshape(B, T, GQA_KV_HEADS, GQA_GROUP, GQA_DIM).transpose(0, 2, 3, 1, 4)
    gqa = (q_g, _hf(k_g), _hf(v_g))
    return (mla, diff, gqa)


def _merge(o_mla, o_diff, o_gqa, lam, lam_init, g_diff_sub):
    B, _, _, T, _ = o_mla.shape
    y_mla = o_mla[:, :, 0].transpose(0, 2, 1, 3).reshape(B, T, MLA_HEADS * MLA_V)
    od = o_diff[:, :, 0].reshape(B, DIFF_HEADS, 2, T, DIFF_V)
    d = od[:, :, 0] - lam.astype(od.dtype) * od[:, :, 1]
    d = _rms(d, g_diff_sub) * (1.0 - lam_init)
    y_diff = d.transpose(0, 2, 1, 3).reshape(B, T, DIFF_HEADS * DIFF_V)
    y_gqa = o_gqa.transpose(0, 3, 1, 2, 4).reshape(B, T, GQA_HEADS * GQA_DIM)
    return jnp.concatenate([y_mla, y_diff, y_gqa], axis=-1)


def _swiglu(h, w_gate, w_up, w_down):
    return (jax.nn.silu(h @ w_gate) * (h @ w_up)) @ w_down


def setup_inputs(seed: int = 0) -> dict:
    key = jax.random.key(seed)
    ks = iter(jax.random.split(key, 32))

    def nrm(shape, scale):
        return jax.random.normal(next(ks), shape, jnp.float32) * scale

    def gain(shape):
        return 1.0 + 0.05 * jax.random.normal(next(ks), shape, jnp.float32)

    D = D_MODEL
    return {
        "x": nrm((BATCH, SEQ, D), 1.0),
        "c": nrm((BATCH, D), 1.0),
        "ctx": nrm((BATCH, CTX_LEN, D), 1.0),
        "c_ctx": nrm((D,), 1.0),
        "w_ada": nrm((DEPTH, D, 6 * D), 0.5 * D ** -0.5),
        "b_ada": nrm((DEPTH, 6 * D), 0.01),
        "g_attn_pre": gain((DEPTH, D)),
        "g_attn_post": gain((DEPTH, D)),
        "w_in": nrm((DEPTH, D, IN_WIDTH), D ** -0.5),
        "g_mla_q": gain((DEPTH, MLA_Q_RANK)),
        "w_mla_qb": nrm((DEPTH, MLA_Q_RANK, MLA_HEADS * (MLA_NOPE + MLA_ROPE)), MLA_Q_RANK ** -0.5),
        "g_mla_kv": gain((DEPTH, MLA_KV_RANK)),
        "w_mla_kvb": nrm((DEPTH, MLA_KV_RANK, MLA_HEADS * (MLA_NOPE + MLA_V)), MLA_KV_RANK ** -0.5),
        "lambda_q1": nrm((DEPTH, DIFF_QK), 0.1),
        "lambda_k1": nrm((DEPTH, DIFF_QK), 0.1),
        "lambda_q2": nrm((DEPTH, DIFF_QK), 0.1),
        "lambda_k2": nrm((DEPTH, DIFF_QK), 0.1),
        "g_diff_sub": gain((DEPTH, DIFF_V)),
        "g_gqa_q": gain((DEPTH, GQA_DIM)),
        "g_gqa_k": gain((DEPTH, GQA_DIM)),
        "w_out": nrm((DEPTH, MIX_WIDTH, D), MIX_WIDTH ** -0.5),
        "g_ffn_pre": gain((DEPTH, D)),
        "g_ffn_post": gain((DEPTH, D)),
        "w_ffn_gate": nrm((DEPTH, D, FFN_HIDDEN), D ** -0.5),
        "w_ffn_up": nrm((DEPTH, D, FFN_HIDDEN), D ** -0.5),
        "w_ffn_down": nrm((DEPTH, FFN_HIDDEN, D), FFN_HIDDEN ** -0.5),
    }


def reference(x, c, ctx, c_ctx, w_ada, b_ada, g_attn_pre, g_attn_post, w_in, g_mla_q, w_mla_qb,
              g_mla_kv, w_mla_kvb, lambda_q1, lambda_k1, lambda_q2, lambda_k2, g_diff_sub,
              g_gqa_q, g_gqa_k, w_out, g_ffn_pre, g_ffn_post, w_ffn_gate, w_ffn_up, w_ffn_down):
    n_tok = x.shape[1]
    rows = n_tok // GRID_W
    row = jnp.repeat(jnp.arange(rows, dtype=jnp.int32), GRID_W)
    col = jnp.tile(jnp.arange(GRID_W, dtype=jnp.int32), rows)
    rope_small = _axial_rope_tables(row, col, MLA_ROPE)
    rope_large = _axial_rope_tables(row, col, GQA_DIM)

    silu_c = jax.nn.silu(c)
    silu_cc = jax.nn.silu(c_ctx)
    xc = ctx
    scales = (MLA_SCALE, DIFF_SCALE, GQA_SCALE)

    for l in range(DEPTH):
        last = l == DEPTH - 1
        mod = (silu_c @ w_ada[l] + b_ada[l])[:, None, :]
        mod_c = (silu_cc @ w_ada[l] + b_ada[l])[None, None, :]
        sh_a, sc_a, gt_a, sh_f, sc_f, gt_f = jnp.split(mod, 6, axis=-1)
        csh_a, csc_a, cgt_a, csh_f, csc_f, cgt_f = jnp.split(mod_c, 6, axis=-1)

        lam_init = 0.8 - 0.6 * math.exp(-0.3 * l)
        lam = (jnp.exp(jnp.sum(lambda_q1[l].astype(jnp.float32) * lambda_k1[l].astype(jnp.float32)))
               - jnp.exp(jnp.sum(lambda_q2[l].astype(jnp.float32) * lambda_k2[l].astype(jnp.float32)))
               + lam_init)

        h = _rms(x, g_attn_pre[l]) * (1.0 + sc_a) + sh_a
        hc = _rms(xc, g_attn_pre[l]) * (1.0 + csc_a) + csh_a
        proj_args = (w_in[l], g_mla_q[l], w_mla_qb[l], g_mla_kv[l], w_mla_kvb[l], g_gqa_q[l], g_gqa_k[l])
        lat = _project(h, *proj_args, rope_small, rope_large)
        cxt = _project(hc, *proj_args, None, None)

        o_lat = []
        for (q, k, v), (qc, kc, vc), s in zip(lat, cxt, scales):
            k_all = jnp.concatenate([k, kc], axis=2)
            v_all = jnp.concatenate([v, vc], axis=2)
            o_lat.append(_latent_attention(q, k_all, v_all, s))
        y = _merge(o_lat[0], o_lat[1], o_lat[2], lam, lam_init, g_diff_sub[l]) @ w_out[l]
        x = x + gt_a * _rms(y, g_attn_post[l])

        if not last:
            o_ctx = [_sdpa(qc, kc, vc, s) for (qc, kc, vc), s in zip(cxt, scales)]
            yc = _merge(o_ctx[0], o_ctx[1], o_ctx[2], lam, lam_init, g_diff_sub[l]) @ w_out[l]
            xc = xc + cgt_a * _rms(yc, g_attn_post[l])

        hf = _rms(x, g_ffn_pre[l]) * (1.0 + sc_f) + sh_f
        x = x + gt_f * _rms(_swiglu(hf, w_ffn_gate[l], w_ffn_up[l], w_ffn_down[l]), g_ffn_post[l])
        if not last:
            hfc = _rms(xc, g_ffn_pre[l]) * (1.0 + csc_f) + csh_f
            xc = xc + cgt_f * _rms(_swiglu(hfc, w_ffn_gate[l], w_ffn_up[l], w_ffn_down[l]), g_ffn_post[l])

    return x
```

```python
import functools
import math

import jax
import jax.numpy as jnp
from jax import lax
from jax.experimental import pallas as pl
from jax.experimental.pallas import tpu as pltpu

F32 = jnp.float32
BF16 = jnp.bfloat16

D_MODEL = 1024
SEQ = 8192
GRID_W = 64
CTX_LEN = 256
T_ALL = SEQ + CTX_LEN
ROPE_THETA = 10000.0
EPS = 1e-6

MLA_HEADS = 4
MLA_Q_RANK = 256
MLA_KV_RANK = 128
MLA_NOPE = 64
MLA_ROPE = 32
MLA_V = 64
DIFF_HEADS = 4
DIFF_QK = 32
DIFF_V = 64
GQA_HEADS = 8
GQA_KV_HEADS = 2
GQA_DIM = 64
HEAD_V = 64
MIX_WIDTH = 1024
FFN_HIDDEN = 2816

LOG2E = 1.4426950408889634
MLA_QS = LOG2E / math.sqrt(MLA_NOPE + MLA_ROPE)
DIFF_QS = LOG2E / math.sqrt(DIFF_QK)
GQA_QS = LOG2E / math.sqrt(GQA_DIM)

LANES = 128
BF16_SUBLANES = 16
VMEM_LIMIT_BYTES = 56 * 1024 * 1024

N_SCORE_HEADS = 20
N_KEY_SLABS = 7
N_VALUE_HEADS = 10
V_ROWS = HEAD_V + BF16_SUBLANES
KEY_SLAB_OF_HEAD = (0, 1, 2, 3, 4, 4, 4, 4, 5, 5, 5, 5, 6, 6, 6, 6, 6, 6, 6, 6)
VALUE_HEAD_OF_HEAD = (0, 1, 2, 3, 4, 4, 5, 5, 6, 6, 7, 7, 8, 8, 8, 8, 9, 9, 9, 9)

TM = 256
TQ = 512
TK = 768

WK_COLS = 1152
WQ_ROWS = 2304


def _rms_rows(x, eps=EPS):
    return lax.rsqrt(jnp.mean(x * x, axis=-1, keepdims=True) + eps)


def _rms_cols(x, eps=EPS):
    return lax.rsqrt(jnp.mean(x * x, axis=0, keepdims=True) + eps)


def _adaln_kernel(c_ref, w_ref, b_ref, o_ref):
    c = c_ref[...]
    sc = c * jax.nn.sigmoid(c)
    o_ref[0] = jnp.dot(sc.astype(BF16), w_ref[0].astype(BF16),
                       preferred_element_type=F32) + b_ref[0]


def _adaln(cvec, w_ada, b_ada):
    depth = w_ada.shape[0]
    tn = 1024
    n_out = w_ada.shape[2]
    return pl.pallas_call(
        _adaln_kernel,
        grid=(depth, n_out // tn),
        in_specs=[
            pl.BlockSpec((8, D_MODEL), lambda l, j: (0, 0)),
            pl.BlockSpec((1, D_MODEL, tn), lambda l, j: (l, 0, j)),
            pl.BlockSpec((1, 1, tn), lambda l, j: (l, 0, j)),
        ],
        out_specs=pl.BlockSpec((1, 8, tn), lambda l, j: (l, 0, j)),
        out_shape=jax.ShapeDtypeStruct((depth, 8, n_out), F32),
        compiler_params=pltpu.CompilerParams(
            dimension_semantics=("arbitrary", "arbitrary"),
            vmem_limit_bytes=VMEM_LIMIT_BYTES),
        name="adaln",
    )(cvec, w_ada, b_ada.reshape(depth, 1, n_out))


def _proj_kernel(x_ref, mod_ref, gpre_ref, wk_ref, wqT_ref, wqbT_ref, wkvk_ref, wkvvT_ref,
                 gqmla_ref, gkv_tok_ref, gkvT_ref, ggq_ref, ggqrot_ref, ggk_ref, ggkrot_ref,
                 cS_ref, sS_ref, cL_ref, sL_ref, cST_ref, sST_ref, cLT_ref, sLT_ref,
                 qT_ref, k_ref, vT_ref):
    tm = x_ref.shape[1]
    x = x_ref[0]
    mod = mod_ref[0]
    sh = mod[:, 0:D_MODEL]
    sc = mod[:, D_MODEL:2 * D_MODEL]
    h = (x * _rms_rows(x) * gpre_ref[...]) * (1.0 + sc) + sh
    hb = h.astype(BF16)
    pk = jnp.dot(hb, wk_ref[...], preferred_element_type=F32)
    pT = lax.dot_general(wqT_ref[...], hb, (((1,), (1,)), ((), ())),
                         preferred_element_type=F32)

    cS = cS_ref[...]
    sS = sS_ref[...]
    cL = cL_ref[...]
    sL = sL_ref[...]
    cST = cST_ref[...]
    sST = sST_ref[...]
    cLT = cLT_ref[...]
    sLT = sLT_ref[...]
    ones_rows = jnp.ones((BF16_SUBLANES, tm), BF16)
    zeros32 = jnp.zeros((32, tm), BF16)
    zeros64 = jnp.zeros((64, tm), BF16)

    qaT = pT[0:256]
    qn = (qaT * _rms_cols(qaT) * gqmla_ref[...]).astype(BF16)
    qm = jnp.dot(wqbT_ref[...], qn, preferred_element_type=F32)
    for hh in range(MLA_HEADS):
        blk = qm[128 * hh:128 * (hh + 1)]
        pe = blk[64:96] * cST + blk[96:128] * sST
        qT_ref[0, hh, 0:64, :] = (blk[0:64] * MLA_QS).astype(BF16)
        qT_ref[0, hh, 64:96, :] = (pe * MLA_QS).astype(BF16)
        qT_ref[0, hh, 96:128, :] = zeros32

    ckv = pk[:, 0:128]
    cn = (ckv * _rms_rows(ckv) * gkv_tok_ref[...]).astype(BF16)
    kn = jnp.dot(cn, wkvk_ref[...], preferred_element_type=F32)
    pe_tok = pk[:, 128:256] * cS[:, 0:128] + pk[:, 256:384] * sS[:, 0:128]
    for hh in range(MLA_HEADS):
        k_ref[0, :, 128 * hh:128 * (hh + 1)] = (kn[:, 128 * hh:128 * (hh + 1)] + pe_tok).astype(BF16)

    ckvT = pT[256:384]
    cnT = (ckvT * _rms_cols(ckvT) * gkvT_ref[...]).astype(BF16)
    vmT = jnp.dot(wkvvT_ref[...], cnT, preferred_element_type=F32)
    for hh in range(MLA_HEADS):
        vT_ref[0, hh, 0:64, :] = vmT[64 * hh:64 * (hh + 1)].astype(BF16)
        vT_ref[0, hh, 64:V_ROWS, :] = ones_rows

    qd = pT[384:640].reshape(8, 32, tm)
    qdr = pT[640:896].reshape(8, 32, tm)
    qd = (qd * cST[None] + qdr * sST[None]) * DIFF_QS
    for j in range(8):
        for rb in range(4):
            val = qd[j].astype(BF16) if rb == j % 4 else zeros32
            qT_ref[0, 4 + j, 32 * rb:32 * (rb + 1), :] = val
    kd = pk[:, 384:640] * cS + pk[:, 640:896] * sS
    k_ref[0, :, 512:768] = kd.astype(BF16)
    vdT = pT[1920:2176]
    for hh in range(DIFF_HEADS):
        vT_ref[0, 4 + hh, 0:64, :] = vdT[64 * hh:64 * (hh + 1)].astype(BF16)
        vT_ref[0, 4 + hh, 64:V_ROWS, :] = ones_rows

    gq = pT[896:1408].reshape(8, 64, tm)
    gqr = pT[1408:1920].reshape(8, 64, tm)
    rq = lax.rsqrt(jnp.mean(gq * gq, axis=1, keepdims=True) + EPS)
    qg = ((gq * rq * ggq_ref[...][None]) * cLT[None]
          + (gqr * rq * ggqrot_ref[...][None]) * sLT[None]) * GQA_QS
    for j in range(8):
        grp = j // 4
        for rb in range(2):
            val = qg[j].astype(BF16) if rb == grp else zeros64
            qT_ref[0, 12 + j, 64 * rb:64 * (rb + 1), :] = val
    gk = pk[:, 896:1024]
    gkr = pk[:, 1024:1152]
    sq = gk * gk
    lane = lax.broadcasted_iota(jnp.int32, sq.shape, 1)
    lo = lane < GQA_DIM
    s0 = jnp.sum(jnp.where(lo, sq, 0.0), axis=-1, keepdims=True)
    s1 = jnp.sum(jnp.where(lo, 0.0, sq), axis=-1, keepdims=True)
    rk = jnp.where(lo, lax.rsqrt(s0 / GQA_DIM + EPS), lax.rsqrt(s1 / GQA_DIM + EPS))
    kg = (gk * rk * ggk_ref[...]) * cL + (gkr * rk * ggkrot_ref[...]) * sL
    k_ref[0, :, 768:896] = kg.astype(BF16)
    gvT = pT[2176:2304]
    for hh in range(GQA_KV_HEADS):
        vT_ref[0, 8 + hh, 0:64, :] = gvT[64 * hh:64 * (hh + 1)].astype(BF16)
        vT_ref[0, 8 + hh, 64:V_ROWS, :] = ones_rows


def _proj(x_all, mod3, lw, tabs):
    bsz, t_all, _ = x_all.shape
    nt = t_all // TM
    const = lambda shape: pl.BlockSpec(shape, lambda b, t: (0,) * len(shape))
    in_specs = [
        pl.BlockSpec((1, TM, D_MODEL), lambda b, t: (b, t, 0)),
        pl.BlockSpec((1, 1, 6 * D_MODEL), lambda b, t: (jnp.where(t == nt - 1, 2, b), 0, 0)),
        const((1, D_MODEL)),
        const((D_MODEL, WK_COLS)),
        const((WQ_ROWS, D_MODEL)),
        const((512, 256)),
        const((128, 512)),
        const((256, 128)),
        const((256, TM)),
        const((1, 128)),
        const((128, TM)),
        const((64, TM)),
        const((64, TM)),
        const((1, 128)),
        const((1, 128)),
        pl.BlockSpec((TM, 256), lambda b, t: (t, 0)),
        pl.BlockSpec((TM, 256), lambda b, t: (t, 0)),
        pl.BlockSpec((TM, 128), lambda b, t: (t, 0)),
        pl.BlockSpec((TM, 128), lambda b, t: (t, 0)),
        pl.BlockSpec((32, TM), lambda b, t: (0, t)),
        pl.BlockSpec((32, TM), lambda b, t: (0, t)),
        pl.BlockSpec((64, TM), lambda b, t: (0, t)),
        pl.BlockSpec((64, TM), lambda b, t: (0, t)),
    ]
    out_specs = [
        pl.BlockSpec((1, N_SCORE_HEADS, 128, TM), lambda b, t: (b, 0, 0, t)),
        pl.BlockSpec((1, TM, N_KEY_SLABS * 128), lambda b, t: (b, t, 0)),
        pl.BlockSpec((1, N_VALUE_HEADS, V_ROWS, TM), lambda b, t: (b, 0, 0, t)),
    ]
    out_shape = [
        jax.ShapeDtypeStruct((bsz, N_SCORE_HEADS, 128, t_all), BF16),
        jax.ShapeDtypeStruct((bsz, t_all, N_KEY_SLABS * 128), BF16),
        jax.ShapeDtypeStruct((bsz, N_VALUE_HEADS, V_ROWS, t_all), BF16),
    ]
    return pl.pallas_call(
        _proj_kernel,
        grid=(bsz, nt),
        in_specs=in_specs,
        out_specs=out_specs,
        out_shape=out_shape,
        compiler_params=pltpu.CompilerParams(
            dimension_semantics=("parallel", "parallel"),
            vmem_limit_bytes=VMEM_LIMIT_BYTES),
        name="proj",
    )(x_all, mod3, lw["g_attn_pre"], lw["wk"], lw["wqT"], lw["wqbT"], lw["wkvk"], lw["wkvvT"],
      lw["gqmla_b"], lw["gkv_tok"], lw["gkvT_b"], lw["ggq_b"], lw["ggqrot_b"], lw["ggk128"],
      lw["ggkrot128"],
      tabs["cS"], tabs["sS"], tabs["cL"], tabs["sL"], tabs["cST"], tabs["sST"], tabs["cLT"],
      tabs["sLT"])


def _attn_kernel(qT_ref, k_ref, vT_ref, lam_ref, gsub_ref, yT_ref, acc_ref, m_ref, *,
                 n_k, lam_init):
    ki = pl.program_id(2)
    tq = qT_ref.shape[3]

    @pl.when(ki == 0)
    def _init():
        acc_ref[...] = jnp.zeros(acc_ref.shape, F32)
        m_ref[...] = jnp.full(m_ref.shape, -jnp.inf, F32)

    for h in range(N_SCORE_HEADS):
        slab = KEY_SLAB_OF_HEAD[h]
        ks = k_ref[0, :, 128 * slab:128 * (slab + 1)]
        sT = jnp.dot(ks, qT_ref[0, h], preferred_element_type=F32)
        m_old = m_ref[h]
        m_new = jnp.maximum(m_old, jnp.max(sT, axis=0, keepdims=True))
        alpha = jnp.exp2(m_old - m_new)
        pT = jnp.exp2(sT - m_new).astype(BF16)
        oT = jnp.dot(vT_ref[0, VALUE_HEAD_OF_HEAD[h]], pT, preferred_element_type=F32)
        acc_ref[h] = acc_ref[h] * alpha + oT
        m_ref[h] = m_new

    @pl.when(ki == n_k - 1)
    def _finalize():
        def head_out(h):
            a = acc_ref[h]
            return a[0:HEAD_V] / a[HEAD_V:HEAD_V + 1]
        for hh in range(MLA_HEADS):
            yT_ref[0, 64 * hh:64 * (hh + 1), :] = head_out(hh).astype(BF16)
        lp = lam_ref[...]
        l1 = jnp.sum(lp[0:1] * lp[1:2], axis=-1, keepdims=True)
        l2 = jnp.sum(lp[2:3] * lp[3:4], axis=-1, keepdims=True)
        lam = jnp.exp(l1) - jnp.exp(l2) + lam_init
        gsub = gsub_ref[...]
        for hh in range(DIFF_HEADS):
            d = head_out(4 + 2 * hh) - lam * head_out(5 + 2 * hh)
            y = (d * _rms_cols(d) * gsub) * (1.0 - lam_init)
            yT_ref[0, 256 + 64 * hh:256 + 64 * (hh + 1), :] = y.astype(BF16)
        for j in range(GQA_HEADS):
            yT_ref[0, 512 + 64 * j:512 + 64 * (j + 1), :] = head_out(12 + j).astype(BF16)


def _attn(qT, kslab, vT, lam_vecs, gsub_b, lam_init, *, tq, tk, q_blocks, q_off, k_blocks, k_off,
          y_prev=None):
    bsz = qT.shape[0]
    t_all = qT.shape[3]
    kern = functools.partial(_attn_kernel, n_k=k_blocks, lam_init=lam_init)
    in_specs = [
        pl.BlockSpec((1, N_SCORE_HEADS, 128, tq), lambda b, qi, ki: (b, 0, 0, qi + q_off)),
        pl.BlockSpec((1, tk, N_KEY_SLABS * 128), lambda b, qi, ki: (b, ki + k_off, 0)),
        pl.BlockSpec((1, N_VALUE_HEADS, V_ROWS, tk), lambda b, qi, ki: (b, 0, 0, ki + k_off)),
        pl.BlockSpec((4, DIFF_QK), lambda b, qi, ki: (0, 0)),
        pl.BlockSpec((HEAD_V, tq), lambda b, qi, ki: (0, 0)),
    ]
    args = [qT, kslab, vT, lam_vecs, gsub_b]
    aliases = {}
    if y_prev is not None:
        in_specs.append(pl.BlockSpec(memory_space=pl.ANY))
        args.append(y_prev)
        aliases = {5: 0}
        kern_fn = lambda q, k, v, lm, gs, yp, y, acc, m: kern(q, k, v, lm, gs, y, acc, m)
    else:
        kern_fn = kern
    return pl.pallas_call(
        kern_fn,
        grid=(bsz, q_blocks, k_blocks),
        in_specs=in_specs,
        out_specs=pl.BlockSpec((1, MIX_WIDTH, tq), lambda b, qi, ki: (b, 0, qi + q_off)),
        out_shape=jax.ShapeDtypeStruct((bsz, MIX_WIDTH, t_all), BF16),
        scratch_shapes=[
            pltpu.VMEM((N_SCORE_HEADS, V_ROWS, tq), F32),
            pltpu.VMEM((N_SCORE_HEADS, 1, tq), F32),
        ],
        input_output_aliases=aliases,
        compiler_params=pltpu.CompilerParams(
            dimension_semantics=("parallel", "parallel", "arbitrary"),
            vmem_limit_bytes=VMEM_LIMIT_BYTES),
        name="attn_ctx" if y_prev is not None else "attn",
    )(*args)


def _out_ffn_kernel(yT_ref, x_ref, mod_ref, gpost_ref, gfpre_ref, gfpost_ref,
                    wout_ref, wg_ref, wu_ref, wd_ref, o_ref):
    yT = yT_ref[0]
    yp = lax.dot_general(yT, wout_ref[...], (((0,), (0,)), ((), ())),
                         preferred_element_type=F32)
    mod = mod_ref[0]
    gt_a = mod[:, 2 * D_MODEL:3 * D_MODEL]
    sh_f = mod[:, 3 * D_MODEL:4 * D_MODEL]
    sc_f = mod[:, 4 * D_MODEL:5 * D_MODEL]
    gt_f = mod[:, 5 * D_MODEL:6 * D_MODEL]
    x1 = x_ref[0] + gt_a * (yp * _rms_rows(yp) * gpost_ref[...])
    hf = (x1 * _rms_rows(x1) * gfpre_ref[...]) * (1.0 + sc_f) + sh_f
    hb = hf.astype(BF16)
    g = jnp.dot(hb, wg_ref[...], preferred_element_type=F32)
    u = jnp.dot(hb, wu_ref[...], preferred_element_type=F32)
    a = (g * jax.nn.sigmoid(g)) * u
    f = jnp.dot(a.astype(BF16), wd_ref[...], preferred_element_type=F32)
    o_ref[0] = x1 + gt_f * (f * _rms_rows(f) * gfpost_ref[...])


def _out_ffn(yT, x_all, mod3, lw, *, n_tiles):
    bsz = x_all.shape[0]
    nt_all = x_all.shape[1] // TM
    const = lambda shape: pl.BlockSpec(shape, lambda b, t: (0,) * len(shape),
                                       pipeline_mode=pl.Buffered(1))
    return pl.pallas_call(
        _out_ffn_kernel,
        grid=(bsz, n_tiles),
        in_specs=[
            pl.BlockSpec((1, MIX_WIDTH, TM), lambda b, t: (b, 0, t)),
            pl.BlockSpec((1, TM, D_MODEL), lambda b, t: (b, t, 0)),
            pl.BlockSpec((1, 1, 6 * D_MODEL), lambda b, t: (jnp.where(t == nt_all - 1, 2, b), 0, 0)),
            const((1, D_MODEL)),
            const((1, D_MODEL)),
            const((1, D_MODEL)),
            const((MIX_WIDTH, D_MODEL)),
            const((D_MODEL, FFN_HIDDEN)),
            const((D_MODEL, FFN_HIDDEN)),
            const((FFN_HIDDEN, D_MODEL)),
        ],
        out_specs=pl.BlockSpec((1, TM, D_MODEL), lambda b, t: (b, t, 0)),
        out_shape=jax.ShapeDtypeStruct((bsz, n_tiles * TM, D_MODEL), F32),
        compiler_params=pltpu.CompilerParams(
            dimension_semantics=("parallel", "parallel"),
            vmem_limit_bytes=VMEM_LIMIT_BYTES),
        name="out_ffn",
    )(yT, x_all, mod3, lw["g_attn_post"], lw["g_ffn_pre"], lw["g_ffn_post"],
      lw["w_out"], lw["w_gate"], lw["w_up"], lw["w_down"])


def _rot_cols(w, half):
    n = w.shape[-1]
    wg = w.reshape(w.shape[:-1] + (n // (2 * half), 2, half))
    return jnp.concatenate([-wg[..., 1:2, :], wg[..., 0:1, :]], axis=-2).reshape(w.shape)


def _swap_halves(g, half):
    return jnp.concatenate([g[half:], g[:half]])


def _rope_tables():
    t = jnp.arange(SEQ, dtype=jnp.int32)
    row = (t // GRID_W).astype(F32)
    col = (t % GRID_W).astype(F32)

    def tables(rot_dim):
        quarter = rot_dim // 4
        inv = ROPE_THETA ** (-jnp.arange(quarter, dtype=F32) / quarter)
        ang = jnp.concatenate([row[:, None] * inv, col[:, None] * inv], axis=-1)
        cos = jnp.concatenate([jnp.cos(ang), jnp.ones((CTX_LEN, rot_dim // 2), F32)], axis=0)
        sin = jnp.concatenate([jnp.sin(ang), jnp.zeros((CTX_LEN, rot_dim // 2), F32)], axis=0)
        return jnp.tile(cos, (1, 2)), jnp.tile(sin, (1, 2))

    c_s, s_s = tables(MLA_ROPE)
    c_l, s_l = tables(GQA_DIM)
    return {
        "cS": jnp.tile(c_s, (1, 8)), "sS": jnp.tile(s_s, (1, 8)),
        "cL": jnp.tile(c_l, (1, 2)), "sL": jnp.tile(s_l, (1, 2)),
        "cST": c_s.T, "sST": s_s.T,
        "cLT": c_l.T, "sLT": s_l.T,
    }


def _layer_weights(l, w_in, g_attn_pre, g_attn_post, g_mla_q, w_mla_qb, g_mla_kv, w_mla_kvb,
                   g_diff_sub, g_gqa_q, g_gqa_k, w_out, g_ffn_pre, g_ffn_post, w_ffn_gate,
                   w_ffn_up, w_ffn_down, lambdas):
    wi = w_in[l]
    q_a = wi[:, 0:256]
    c_kv = wi[:, 256:384]
    k_pe = wi[:, 384:416]
    dq = wi[:, 416:672]
    dk = wi[:, 672:928]
    dv = wi[:, 928:1184]
    gq = wi[:, 1184:1696]
    gk = wi[:, 1696:1824]
    gv = wi[:, 1824:1952]
    zeros = lambda n: jnp.zeros((D_MODEL, n), F32)
    pe_slab = jnp.concatenate([zeros(64), k_pe, zeros(32)], axis=1)
    pe_rot_slab = jnp.concatenate([zeros(64), _rot_cols(k_pe, 16), zeros(32)], axis=1)
    wk = jnp.concatenate([c_kv, pe_slab, pe_rot_slab, dk, _rot_cols(dk, 16), gk, _rot_cols(gk, 32)],
                         axis=1)
    wq = jnp.concatenate([q_a, c_kv, dq, _rot_cols(dq, 16), gq, _rot_cols(gq, 32), dv, gv], axis=1)
    qb = w_mla_qb[l].reshape(MLA_Q_RANK, MLA_HEADS, MLA_NOPE + MLA_ROPE)
    qb_pe = qb[:, :, MLA_NOPE:]
    wqb = jnp.concatenate([qb, _rot_cols(qb_pe, 16)], axis=-1).reshape(MLA_Q_RANK, 512)
    kvb = w_mla_kvb[l].reshape(MLA_KV_RANK, MLA_HEADS, MLA_NOPE + MLA_V)
    wkvk = jnp.concatenate([kvb[:, :, :MLA_NOPE], jnp.zeros((MLA_KV_RANK, MLA_HEADS, 64), F32)],
                           axis=-1).reshape(MLA_KV_RANK, 512)
    wkvv = kvb[:, :, MLA_NOPE:].reshape(MLA_KV_RANK, MLA_HEADS * MLA_V)
    bcast = lambda g, n: jnp.broadcast_to(g[:, None], (g.shape[0], n))
    return {
        "g_attn_pre": g_attn_pre[l][None], "g_attn_post": g_attn_post[l][None],
        "g_ffn_pre": g_ffn_pre[l][None], "g_ffn_post": g_ffn_post[l][None],
        "wk": wk.astype(BF16), "wqT": wq.T.astype(BF16), "wqbT": wqb.T.astype(BF16),
        "wkvk": wkvk.astype(BF16), "wkvvT": wkvv.T.astype(BF16),
        "gqmla_b": bcast(g_mla_q[l], TM), "gkv_tok": g_mla_kv[l][None],
        "gkvT_b": bcast(g_mla_kv[l], TM),
        "ggq_b": bcast(g_gqa_q[l], TM), "ggqrot_b": bcast(_swap_halves(g_gqa_q[l], 32), TM),
        "ggk128": jnp.tile(g_gqa_k[l], 2)[None],
        "ggkrot128": jnp.tile(_swap_halves(g_gqa_k[l], 32), 2)[None],
        "gsub_b": bcast(g_diff_sub[l], TQ),
        "lam_vecs": jnp.stack([lam[l] for lam in lambdas]),
        "w_out": w_out[l].astype(BF16), "w_gate": w_ffn_gate[l].astype(BF16),
        "w_up": w_ffn_up[l].astype(BF16), "w_down": w_ffn_down[l].astype(BF16),
    }


def kernel(x, c, ctx, c_ctx, w_ada, b_ada, g_attn_pre, g_attn_post, w_in, g_mla_q, w_mla_qb, g_mla_kv, w_mla_kvb, lambda_q1, lambda_k1, lambda_q2, lambda_k2, g_diff_sub, g_gqa_q, g_gqa_k, w_out, g_ffn_pre, g_ffn_post, w_ffn_gate, w_ffn_up, w_ffn_down):
    bsz = x.shape[0]
    depth = w_ada.shape[0]
    assert x.shape == (bsz, SEQ, D_MODEL) and ctx.shape == (bsz, CTX_LEN, D_MODEL) and bsz == 2
    tabs = _rope_tables()
    cvec = jnp.concatenate([c, c_ctx[None], jnp.zeros((8 - bsz - 1, D_MODEL), F32)], axis=0)
    mod_all = _adaln(cvec, w_ada, b_ada)
    x_all = jnp.concatenate([x, ctx], axis=1)
    n_lat = SEQ // TM
    for l in range(depth):
        last = l == depth - 1
        lam_init = 0.8 - 0.6 * math.exp(-0.3 * l)
        lw = _layer_weights(l, w_in, g_attn_pre, g_attn_post, g_mla_q, w_mla_qb, g_mla_kv,
                            w_mla_kvb, g_diff_sub, g_gqa_q, g_gqa_k, w_out, g_ffn_pre, g_ffn_post,
                            w_ffn_gate, w_ffn_up, w_ffn_down,
                            (lambda_q1, lambda_k1, lambda_q2, lambda_k2))
        mod3 = mod_all[l].reshape(8, 1, 6 * D_MODEL)
        qT, kslab, vT = _proj(x_all, mod3, lw, tabs)
        yT = _attn(qT, kslab, vT, lw["lam_vecs"], lw["gsub_b"], lam_init, tq=TQ, tk=TK,
                   q_blocks=SEQ // TQ, q_off=0, k_blocks=T_ALL // TK, k_off=0)
        if not last:
            yT = _attn(qT, kslab, vT, lw["lam_vecs"], lw["gsub_b"][:, :CTX_LEN], lam_init,
                       tq=CTX_LEN, tk=CTX_LEN, q_blocks=1, q_off=SEQ // CTX_LEN, k_blocks=1,
                       k_off=SEQ // CTX_LEN, y_prev=yT)
        x_all = _out_ffn(yT, x_all, mod3, lw, n_tiles=n_lat if last else n_lat + 1)
    return x_all
```

```python
import functools
import math

import jax
import jax.numpy as jnp
from jax import lax
from jax.experimental import pallas as pl
from jax.experimental.pallas import tpu as pltpu

F32 = jnp.float32
BF16 = jnp.bfloat16

D_MODEL = 1024
SEQ = 8192
GRID_W = 64
CTX_LEN = 256
T_ALL = SEQ + CTX_LEN
ROPE_THETA = 10000.0
EPS = 1e-6

MLA_HEADS = 4
MLA_Q_RANK = 256
MLA_KV_RANK = 128
MLA_NOPE = 64
MLA_ROPE = 32
MLA_V = 64
DIFF_HEADS = 4
DIFF_QK = 32
DIFF_V = 64
GQA_HEADS = 8
GQA_KV_HEADS = 2
GQA_DIM = 64
HEAD_V = 64
MIX_WIDTH = 1024
FFN_HIDDEN = 2816

LOG2E = 1.4426950408889634
MLA_QS = LOG2E / math.sqrt(MLA_NOPE + MLA_ROPE)
DIFF_QS = LOG2E / math.sqrt(DIFF_QK)
GQA_QS = LOG2E / math.sqrt(GQA_DIM)

LANES = 128
BF16_SUBLANES = 16
VMEM_LIMIT_BYTES = 56 * 1024 * 1024

N_SCORE_HEADS = 20
N_KEY_SLABS = 7
N_VALUE_HEADS = 10
V_ROWS = HEAD_V + BF16_SUBLANES
KEY_SLAB_OF_HEAD = (0, 1, 2, 3, 4, 4, 4, 4, 5, 5, 5, 5, 6, 6, 6, 6, 6, 6, 6, 6)
VALUE_HEAD_OF_HEAD = (0, 1, 2, 3, 4, 4, 5, 5, 6, 6, 7, 7, 8, 8, 8, 8, 9, 9, 9, 9)

TM = 256
TQ = 512
TK = 768

WK_COLS = 1152
WQ_ROWS = 2304


def _rms_rows(x, eps=EPS):
    return lax.rsqrt(jnp.mean(x * x, axis=-1, keepdims=True) + eps)


def _rms_cols(x, eps=EPS):
    return lax.rsqrt(jnp.mean(x * x, axis=0, keepdims=True) + eps)


def _adaln_kernel(c_ref, w_ref, b_ref, o_ref):
    c = c_ref[...]
    sc = c * jax.nn.sigmoid(c)
    o_ref[0] = jnp.dot(sc.astype(BF16), w_ref[0].astype(BF16),
                       preferred_element_type=F32) + b_ref[0]


def _adaln(cvec, w_ada, b_ada):
    depth = w_ada.shape[0]
    tn = 1024
    n_out = w_ada.shape[2]
    return pl.pallas_call(
        _adaln_kernel,
        grid=(depth, n_out // tn),
        in_specs=[
            pl.BlockSpec((8, D_MODEL), lambda l, j: (0, 0)),
            pl.BlockSpec((1, D_MODEL, tn), lambda l, j: (l, 0, j)),
            pl.BlockSpec((1, 1, tn), lambda l, j: (l, 0, j)),
        ],
        out_specs=pl.BlockSpec((1, 8, tn), lambda l, j: (l, 0, j)),
        out_shape=jax.ShapeDtypeStruct((depth, 8, n_out), F32),
        compiler_params=pltpu.CompilerParams(
            dimension_semantics=("arbitrary", "arbitrary"),
            vmem_limit_bytes=VMEM_LIMIT_BYTES),
        name="adaln",
    )(cvec, w_ada, b_ada.reshape(depth, 1, n_out))


def _proj_kernel(x_ref, mod_ref, gpre_ref, wk_ref, wqT_ref, wqbT_ref, wkvk_ref, wkvvT_ref,
                 gqmla_ref, gkv_tok_ref, gkvT_ref, ggq_ref, ggqrot_ref, ggk_ref, ggkrot_ref,
                 cS_ref, sS_ref, cL_ref, sL_ref, cST_ref, sST_ref, cLT_ref, sLT_ref,
                 qT_ref, k_ref, vT_ref):
    tm = x_ref.shape[1]
    x = x_ref[0]
    mod = mod_ref[0]
    sh = mod[:, 0:D_MODEL]
    sc = mod[:, D_MODEL:2 * D_MODEL]
    h = (x * _rms_rows(x) * gpre_ref[...]) * (1.0 + sc) + sh
    hb = h.astype(BF16)
    pk = jnp.dot(hb, wk_ref[...], preferred_element_type=F32)
    pT = lax.dot_general(wqT_ref[...], hb, (((1,), (1,)), ((), ())),
                         preferred_element_type=F32)

    cS = cS_ref[...]
    sS = sS_ref[...]
    cL = cL_ref[...]
    sL = sL_ref[...]
    cST = cST_ref[...]
    sST = sST_ref[...]
    cLT = cLT_ref[...]
    sLT = sLT_ref[...]
    ones_rows = jnp.ones((BF16_SUBLANES, tm), BF16)
    zeros32 = jnp.zeros((32, tm), BF16)
    zeros64 = jnp.zeros((64, tm), BF16)

    qaT = pT[0:256]
    qn = (qaT * _rms_cols(qaT) * gqmla_ref[...]).astype(BF16)
    qm = jnp.dot(wqbT_ref[...], qn, preferred_element_type=F32)
    for hh in range(MLA_HEADS):
        blk = qm[128 * hh:128 * (hh + 1)]
        pe = blk[64:96] * cST + blk[96:128] * sST
        qT_ref[0, hh, 0:64, :] = (blk[0:64] * MLA_QS).astype(BF16)
        qT_ref[0, hh, 64:96, :] = (pe * MLA_QS).astype(BF16)
        qT_ref[0, hh, 96:128, :] = zeros32

    ckv = pk[:, 0:128]
    cn = (ckv * _rms_rows(ckv) * gkv_tok_ref[...]).astype(BF16)
    kn = jnp.dot(cn, wkvk_ref[...], preferred_element_type=F32)
    pe_tok = pk[:, 128:256] * cS[:, 0:128] + pk[:, 256:384] * sS[:, 0:128]
    for hh in range(MLA_HEADS):
        k_ref[0, :, 128 * hh:128 * (hh + 1)] = (kn[:, 128 * hh:128 * (hh + 1)] + pe_tok).astype(BF16)

    ckvT = pT[256:384]
    cnT = (ckvT * _rms_cols(ckvT) * gkvT_ref[...]).astype(BF16)
    vmT = jnp.dot(wkvvT_ref[...], cnT, preferred_element_type=F32)
    for hh in range(MLA_HEADS):
        vT_ref[0, hh, 0:64, :] = vmT[64 * hh:64 * (hh + 1)].astype(BF16)
        vT_ref[0, hh, 64:V_ROWS, :] = ones_rows

    qd = pT[384:640].reshape(8, 32, tm)
    qdr = pT[640:896].reshape(8, 32, tm)
    qd = (qd * cST[None] + qdr * sST[None]) * DIFF_QS
    for j in range(8):
        for rb in range(4):
            val = qd[j].astype(BF16) if rb == j % 4 else zeros32
            qT_ref[0, 4 + j, 32 * rb:32 * (rb + 1), :] = val
    kd = pk[:, 384:640] * cS + pk[:, 640:896] * sS
    k_ref[0, :, 512:768] = kd.astype(BF16)
    vdT = pT[1920:2176]
    for hh in range(DIFF_HEADS):
        vT_ref[0, 4 + hh, 0:64, :] = vdT[64 * hh:64 * (hh + 1)].astype(BF16)
        vT_ref[0, 4 + hh, 64:V_ROWS, :] = ones_rows

    gq = pT[896:1408].reshape(8, 64, tm)
    gqr = pT[1408:1920].reshape(8, 64, tm)
    rq = lax.rsqrt(jnp.mean(gq * gq, axis=1, keepdims=True) + EPS)
    qg = ((gq * rq * ggq_ref[...][None]) * cLT[None]
          + (gqr * rq * ggqrot_ref[...][None]) * sLT[None]) * GQA_QS
    for j in range(8):
        grp = j // 4
        for rb in range(2):
            val = qg[j].astype(BF16) if rb == grp else zeros64
            qT_ref[0, 12 + j, 64 * rb:64 * (rb + 1), :] = val
    gk = pk[:, 896:1024]
    gkr = pk[:, 1024:1152]
    sq = gk * gk
    lane = lax.broadcasted_iota(jnp.int32, sq.shape, 1)
    lo = lane < GQA_DIM
    s0 = jnp.sum(jnp.where(lo, sq, 0.0), axis=-1, keepdims=True)
    s1 = jnp.sum(jnp.where(lo, 0.0, sq), axis=-1, keepdims=True)
    rk = jnp.where(lo, lax.rsqrt(s0 / GQA_DIM + EPS), lax.rsqrt(s1 / GQA_DIM + EPS))
    kg = (gk * rk * ggk_ref[...]) * cL + (gkr * rk * ggkrot_ref[...]) * sL
    k_ref[0, :, 768:896] = kg.astype(BF16)
    gvT = pT[2176:2304]
    for hh in range(GQA_KV_HEADS):
        vT_ref[0, 8 + hh, 0:64, :] = gvT[64 * hh:64 * (hh + 1)].astype(BF16)
        vT_ref[0, 8 + hh, 64:V_ROWS, :] = ones_rows


def _proj(x_all, mod3, lw, tabs):
    bsz, t_all, _ = x_all.shape
    nt = t_all // TM
    const = lambda shape: pl.BlockSpec(shape, lambda b, t: (0,) * len(shape))
    in_specs = [
        pl.BlockSpec((1, TM, D_MODEL), lambda b, t: (b, t, 0)),
        pl.BlockSpec((1, 1, 6 * D_MODEL), lambda b, t: (jnp.where(t == nt - 1, 2, b), 0, 0)),
        const((1, D_MODEL)),
        const((D_MODEL, WK_COLS)),
        const((WQ_ROWS, D_MODEL)),
        const((512, 256)),
        const((128, 512)),
        const((256, 128)),
        const((256, TM)),
        const((1, 128)),
        const((128, TM)),
        const((64, TM)),
        const((64, TM)),
        const((1, 128)),
        const((1, 128)),
        pl.BlockSpec((TM, 256), lambda b, t: (t, 0)),
        pl.BlockSpec((TM, 256), lambda b, t: (t, 0)),
        pl.BlockSpec((TM, 128), lambda b, t: (t, 0)),
        pl.BlockSpec((TM, 128), lambda b, t: (t, 0)),
        pl.BlockSpec((32, TM), lambda b, t: (0, t)),
        pl.BlockSpec((32, TM), lambda b, t: (0, t)),
        pl.BlockSpec((64, TM), lambda b, t: (0, t)),
        pl.BlockSpec((64, TM), lambda b, t: (0, t)),
    ]
    out_specs = [
        pl.BlockSpec((1, N_SCORE_HEADS, 128, TM), lambda b, t: (b, 0, 0, t)),
        pl.BlockSpec((1, TM, N_KEY_SLABS * 128), lambda b, t: (b, t, 0)),
        pl.BlockSpec((1, N_VALUE_HEADS, V_ROWS, TM), lambda b, t: (b, 0, 0, t)),
    ]
    out_shape = [
        jax.ShapeDtypeStruct((bsz, N_SCORE_HEADS, 128, t_all), BF16),
        jax.ShapeDtypeStruct((bsz, t_all, N_KEY_SLABS * 128), BF16),
        jax.ShapeDtypeStruct((bsz, N_VALUE_HEADS, V_ROWS, t_all), BF16),
    ]
    return pl.pallas_call(
        _proj_kernel,
        grid=(bsz, nt),
        in_specs=in_specs,
        out_specs=out_specs,
        out_shape=out_shape,
        compiler_params=pltpu.CompilerParams(
            dimension_semantics=("parallel", "parallel"),
            vmem_limit_bytes=VMEM_LIMIT_BYTES),
        name="proj",
    )(x_all, mod3, lw["g_attn_pre"], lw["wk"], lw["wqT"], lw["wqbT"], lw["wkvk"], lw["wkvvT"],
      lw["gqmla_b"], lw["gkv_tok"], lw["gkvT_b"], lw["ggq_b"], lw["ggqrot_b"], lw["ggk128"],
      lw["ggkrot128"],
      tabs["cS"], tabs["sS"], tabs["cL"], tabs["sL"], tabs["cST"], tabs["sST"], tabs["cLT"],
      tabs["sLT"])


def _attn_kernel(qT_ref, k_ref, vT_ref, lam_ref, gsub_ref, yT_ref, acc_ref, m_ref, *,
                 n_k, lam_init):
    ki = pl.program_id(2)
    tq = qT_ref.shape[3]

    @pl.when(ki == 0)
    def _init():
        acc_ref[...] = jnp.zeros(acc_ref.shape, F32)
        m_ref[...] = jnp.full(m_ref.shape, -jnp.inf, F32)

    def scores(h):
        slab = KEY_SLAB_OF_HEAD[h]
        ks = k_ref[0, :, 128 * slab:128 * (slab + 1)]
        return jnp.dot(ks, qT_ref[0, h], preferred_element_type=F32)

    def weighted_values(h, pT, alpha):
        oT = jnp.dot(vT_ref[0, VALUE_HEAD_OF_HEAD[h]], pT, preferred_element_type=F32)
        acc_ref[h] = acc_ref[h] * alpha + oT

    lookahead = 2
    pending = [scores(h) for h in range(lookahead)]
    for h in range(N_SCORE_HEADS):
        sT = pending.pop(0)
        if h + lookahead < N_SCORE_HEADS:
            pending.append(scores(h + lookahead))
        m_old = m_ref[h]
        m_new = jnp.maximum(m_old, jnp.max(sT, axis=0, keepdims=True))
        alpha = jnp.exp2(m_old - m_new)
        weighted_values(h, jnp.exp2(sT - m_new).astype(BF16), alpha)
        m_ref[h] = m_new

    @pl.when(ki == n_k - 1)
    def _finalize():
        def head_out(h):
            a = acc_ref[h]
            return a[0:HEAD_V] / a[HEAD_V:HEAD_V + 1]
        for hh in range(MLA_HEADS):
            yT_ref[0, 64 * hh:64 * (hh + 1), :] = head_out(hh).astype(BF16)
        lp = lam_ref[...]
        l1 = jnp.sum(lp[0:1] * lp[1:2], axis=-1, keepdims=True)
        l2 = jnp.sum(lp[2:3] * lp[3:4], axis=-1, keepdims=True)
        lam = jnp.exp(l1) - jnp.exp(l2) + lam_init
        gsub = gsub_ref[...]
        for hh in range(DIFF_HEADS):
            d = head_out(4 + 2 * hh) - lam * head_out(5 + 2 * hh)
            y = (d * _rms_cols(d) * gsub) * (1.0 - lam_init)
            yT_ref[0, 256 + 64 * hh:256 + 64 * (hh + 1), :] = y.astype(BF16)
        for j in range(GQA_HEADS):
            yT_ref[0, 512 + 64 * j:512 + 64 * (j + 1), :] = head_out(12 + j).astype(BF16)


def _attn(qT, kslab, vT, lam_vecs, gsub_b, lam_init, *, tq, tk, q_blocks, q_off, k_blocks, k_off,
          y_prev=None):
    bsz = qT.shape[0]
    t_all = qT.shape[3]
    kern = functools.partial(_attn_kernel, n_k=k_blocks, lam_init=lam_init)
    in_specs = [
        pl.BlockSpec((1, N_SCORE_HEADS, 128, tq), lambda b, qi, ki: (b, 0, 0, qi + q_off)),
        pl.BlockSpec((1, tk, N_KEY_SLABS * 128), lambda b, qi, ki: (b, ki + k_off, 0)),
        pl.BlockSpec((1, N_VALUE_HEADS, V_ROWS, tk), lambda b, qi, ki: (b, 0, 0, ki + k_off)),
        pl.BlockSpec((4, DIFF_QK), lambda b, qi, ki: (0, 0)),
        pl.BlockSpec((HEAD_V, tq), lambda b, qi, ki: (0, 0)),
    ]
    args = [qT, kslab, vT, lam_vecs, gsub_b]
    aliases = {}
    if y_prev is not None:
        in_specs.append(pl.BlockSpec(memory_space=pl.ANY))
        args.append(y_prev)
        aliases = {5: 0}
        kern_fn = lambda q, k, v, lm, gs, yp, y, acc, m: kern(q, k, v, lm, gs, y, acc, m)
    else:
        kern_fn = kern
    return pl.pallas_call(
        kern_fn,
        grid=(bsz, q_blocks, k_blocks),
        in_specs=in_specs,
        out_specs=pl.BlockSpec((1, MIX_WIDTH, tq), lambda b, qi, ki: (b, 0, qi + q_off)),
        out_shape=jax.ShapeDtypeStruct((bsz, MIX_WIDTH, t_all), BF16),
        scratch_shapes=[
            pltpu.VMEM((N_SCORE_HEADS, V_ROWS, tq), F32),
            pltpu.VMEM((N_SCORE_HEADS, 1, tq), F32),
        ],
        input_output_aliases=aliases,
        compiler_params=pltpu.CompilerParams(
            dimension_semantics=("parallel", "parallel", "arbitrary"),
            vmem_limit_bytes=VMEM_LIMIT_BYTES),
        name="attn_ctx" if y_prev is not None else "attn",
    )(*args)


def _out_ffn_kernel(yT_ref, x_ref, mod_ref, gpost_ref, gfpre_ref, gfpost_ref,
                    wout_ref, wg_ref, wu_ref, wd_ref, o_ref):
    yT = yT_ref[0]
    yp = lax.dot_general(yT, wout_ref[...], (((0,), (0,)), ((), ())),
                         preferred_element_type=F32)
    mod = mod_ref[0]
    gt_a = mod[:, 2 * D_MODEL:3 * D_MODEL]
    sh_f = mod[:, 3 * D_MODEL:4 * D_MODEL]
    sc_f = mod[:, 4 * D_MODEL:5 * D_MODEL]
    gt_f = mod[:, 5 * D_MODEL:6 * D_MODEL]
    x1 = x_ref[0] + gt_a * (yp * _rms_rows(yp) * gpost_ref[...])
    hf = (x1 * _rms_rows(x1) * gfpre_ref[...]) * (1.0 + sc_f) + sh_f
    hb = hf.astype(BF16)
    g = jnp.dot(hb, wg_ref[...], preferred_element_type=F32)
    u = jnp.dot(hb, wu_ref[...], preferred_element_type=F32)
    a = (g * jax.nn.sigmoid(g)) * u
    f = jnp.dot(a.astype(BF16), wd_ref[...], preferred_element_type=F32)
    o_ref[0] = x1 + gt_f * (f * _rms_rows(f) * gfpost_ref[...])


def _out_ffn(yT, x_all, mod3, lw, *, n_tiles):
    bsz = x_all.shape[0]
    nt_all = x_all.shape[1] // TM
    const = lambda shape: pl.BlockSpec(shape, lambda b, t: (0,) * len(shape),
                                       pipeline_mode=pl.Buffered(1))
    return pl.pallas_call(
        _out_ffn_kernel,
        grid=(bsz, n_tiles),
        in_specs=[
            pl.BlockSpec((1, MIX_WIDTH, TM), lambda b, t: (b, 0, t)),
            pl.BlockSpec((1, TM, D_MODEL), lambda b, t: (b, t, 0)),
            pl.BlockSpec((1, 1, 6 * D_MODEL), lambda b, t: (jnp.where(t == nt_all - 1, 2, b), 0, 0)),
            const((1, D_MODEL)),
            const((1, D_MODEL)),
            const((1, D_MODEL)),
            const((MIX_WIDTH, D_MODEL)),
            const((D_MODEL, FFN_HIDDEN)),
            const((D_MODEL, FFN_HIDDEN)),
            const((FFN_HIDDEN, D_MODEL)),
        ],
        out_specs=pl.BlockSpec((1, TM, D_MODEL), lambda b, t: (b, t, 0)),
        out_shape=jax.ShapeDtypeStruct((bsz, n_tiles * TM, D_MODEL), F32),
        compiler_params=pltpu.CompilerParams(
            dimension_semantics=("parallel", "parallel"),
            vmem_limit_bytes=VMEM_LIMIT_BYTES),
        name="out_ffn",
    )(yT, x_all, mod3, lw["g_attn_post"], lw["g_ffn_pre"], lw["g_ffn_post"],
      lw["w_out"], lw["w_gate"], lw["w_up"], lw["w_down"])


def _rot_cols(w, half):
    n = w.shape[-1]
    wg = w.reshape(w.shape[:-1] + (n // (2 * half), 2, half))
    return jnp.concatenate([-wg[..., 1:2, :], wg[..., 0:1, :]], axis=-2).reshape(w.shape)


def _swap_halves(g, half):
    return jnp.concatenate([g[half:], g[:half]])


def _rope_tables():
    t = jnp.arange(SEQ, dtype=jnp.int32)
    row = (t // GRID_W).astype(F32)
    col = (t % GRID_W).astype(F32)

    def tables(rot_dim):
        quarter = rot_dim // 4
        inv = ROPE_THETA ** (-jnp.arange(quarter, dtype=F32) / quarter)
        ang = jnp.concatenate([row[:, None] * inv, col[:, None] * inv], axis=-1)
        cos = jnp.concatenate([jnp.cos(ang), jnp.ones((CTX_LEN, rot_dim // 2), F32)], axis=0)
        sin = jnp.concatenate([jnp.sin(ang), jnp.zeros((CTX_LEN, rot_dim // 2), F32)], axis=0)
        return jnp.tile(cos, (1, 2)), jnp.tile(sin, (1, 2))

    c_s, s_s = tables(MLA_ROPE)
    c_l, s_l = tables(GQA_DIM)
    return {
        "cS": jnp.tile(c_s, (1, 8)), "sS": jnp.tile(s_s, (1, 8)),
        "cL": jnp.tile(c_l, (1, 2)), "sL": jnp.tile(s_l, (1, 2)),
        "cST": c_s.T, "sST": s_s.T,
        "cLT": c_l.T, "sLT": s_l.T,
    }


def _layer_weights(l, w_in, g_attn_pre, g_attn_post, g_mla_q, w_mla_qb, g_mla_kv, w_mla_kvb,
                   g_diff_sub, g_gqa_q, g_gqa_k, w_out, g_ffn_pre, g_ffn_post, w_ffn_gate,
                   w_ffn_up, w_ffn_down, lambdas):
    wi = w_in[l]
    q_a = wi[:, 0:256]
    c_kv = wi[:, 256:384]
    k_pe = wi[:, 384:416]
    dq = wi[:, 416:672]
    dk = wi[:, 672:928]
    dv = wi[:, 928:1184]
    gq = wi[:, 1184:1696]
    gk = wi[:, 1696:1824]
    gv = wi[:, 1824:1952]
    zeros = lambda n: jnp.zeros((D_MODEL, n), F32)
    pe_slab = jnp.concatenate([zeros(64), k_pe, zeros(32)], axis=1)
    pe_rot_slab = jnp.concatenate([zeros(64), _rot_cols(k_pe, 16), zeros(32)], axis=1)
    wk = jnp.concatenate([c_kv, pe_slab, pe_rot_slab, dk, _rot_cols(dk, 16), gk, _rot_cols(gk, 32)],
                         axis=1)
    wq = jnp.concatenate([q_a, c_kv, dq, _rot_cols(dq, 16), gq, _rot_cols(gq, 32), dv, gv], axis=1)
    qb = w_mla_qb[l].reshape(MLA_Q_RANK, MLA_HEADS, MLA_NOPE + MLA_ROPE)
    qb_pe = qb[:, :, MLA_NOPE:]
    wqb = jnp.concatenate([qb, _rot_cols(qb_pe, 16)], axis=-1).reshape(MLA_Q_RANK, 512)
    kvb = w_mla_kvb[l].reshape(MLA_KV_RANK, MLA_HEADS, MLA_NOPE + MLA_V)
    wkvk = jnp.concatenate([kvb[:, :, :MLA_NOPE], jnp.zeros((MLA_KV_RANK, MLA_HEADS, 64), F32)],
                           axis=-1).reshape(MLA_KV_RANK, 512)
    wkvv = kvb[:, :, MLA_NOPE:].reshape(MLA_KV_RANK, MLA_HEADS * MLA_V)
    bcast = lambda g, n: jnp.broadcast_to(g[:, None], (g.shape[0], n))
    return {
        "g_attn_pre": g_attn_pre[l][None], "g_attn_post": g_attn_post[l][None],
        "g_ffn_pre": g_ffn_pre[l][None], "g_ffn_post": g_ffn_post[l][None],
        "wk": wk.astype(BF16), "wqT": wq.T.astype(BF16), "wqbT": wqb.T.astype(BF16),
        "wkvk": wkvk.astype(BF16), "wkvvT": wkvv.T.astype(BF16),
        "gqmla_b": bcast(g_mla_q[l], TM), "gkv_tok": g_mla_kv[l][None],
        "gkvT_b": bcast(g_mla_kv[l], TM),
        "ggq_b": bcast(g_gqa_q[l], TM), "ggqrot_b": bcast(_swap_halves(g_gqa_q[l], 32), TM),
        "ggk128": jnp.tile(g_gqa_k[l], 2)[None],
        "ggkrot128": jnp.tile(_swap_halves(g_gqa_k[l], 32), 2)[None],
        "gsub_b": bcast(g_diff_sub[l], TQ),
        "lam_vecs": jnp.stack([lam[l] for lam in lambdas]),
        "w_out": w_out[l].astype(BF16), "w_gate": w_ffn_gate[l].astype(BF16),
        "w_up": w_ffn_up[l].astype(BF16), "w_down": w_ffn_down[l].astype(BF16),
    }


def kernel(x, c, ctx, c_ctx, w_ada, b_ada, g_attn_pre, g_attn_post, w_in, g_mla_q, w_mla_qb, g_mla_kv, w_mla_kvb, lambda_q1, lambda_k1, lambda_q2, lambda_k2, g_diff_sub, g_gqa_q, g_gqa_k, w_out, g_ffn_pre, g_ffn_post, w_ffn_gate, w_ffn_up, w_ffn_down):
    bsz = x.shape[0]
    depth = w_ada.shape[0]
    assert x.shape == (bsz, SEQ, D_MODEL) and ctx.shape == (bsz, CTX_LEN, D_MODEL) and bsz == 2
    tabs = _rope_tables()
    cvec = jnp.concatenate([c, c_ctx[None], jnp.zeros((8 - bsz - 1, D_MODEL), F32)], axis=0)
    mod_all = _adaln(cvec, w_ada, b_ada)
    x_all = jnp.concatenate([x, ctx], axis=1)
    n_lat = SEQ // TM
    for l in range(depth):
        last = l == depth - 1
        lam_init = 0.8 - 0.6 * math.exp(-0.3 * l)
        lw = _layer_weights(l, w_in, g_attn_pre, g_attn_post, g_mla_q, w_mla_qb, g_mla_kv,
                            w_mla_kvb, g_diff_sub, g_gqa_q, g_gqa_k, w_out, g_ffn_pre, g_ffn_post,
                            w_ffn_gate, w_ffn_up, w_ffn_down,
                            (lambda_q1, lambda_k1, lambda_q2, lambda_k2))
        mod3 = mod_all[l].reshape(8, 1, 6 * D_MODEL)
        qT, kslab, vT = _proj(x_all, mod3, lw, tabs)
        yT = _attn(qT, kslab, vT, lw["lam_vecs"], lw["gsub_b"], lam_init, tq=TQ, tk=TK,
                   q_blocks=SEQ // TQ, q_off=0, k_blocks=T_ALL // TK, k_off=0)
        if not last:
            yT = _attn(qT, kslab, vT, lw["lam_vecs"], lw["gsub_b"][:, :CTX_LEN], lam_init,
                       tq=CTX_LEN, tk=CTX_LEN, q_blocks=1, q_off=SEQ // CTX_LEN, k_blocks=1,
                       k_off=SEQ // CTX_LEN, y_prev=yT)
        x_all = _out_ffn(yT, x_all, mod3, lw, n_tiles=n_lat if last else n_lat + 1)
    return x_all
```

```python
import functools
import math

import jax
import jax.numpy as jnp
from jax import lax
from jax.experimental import pallas as pl
from jax.experimental.pallas import tpu as pltpu

F32 = jnp.float32
BF16 = jnp.bfloat16

D_MODEL = 1024
SEQ = 8192
GRID_W = 64
CTX_LEN = 256
T_ALL = SEQ + CTX_LEN
ROPE_THETA = 10000.0
EPS = 1e-6

MLA_HEADS = 4
MLA_Q_RANK = 256
MLA_KV_RANK = 128
MLA_NOPE = 64
MLA_ROPE = 32
MLA_V = 64
DIFF_HEADS = 4
DIFF_QK = 32
DIFF_V = 64
GQA_HEADS = 8
GQA_KV_HEADS = 2
GQA_DIM = 64
HEAD_V = 64
MIX_WIDTH = 1024
FFN_HIDDEN = 2816

LOG2E = 1.4426950408889634
MLA_QS = LOG2E / math.sqrt(MLA_NOPE + MLA_ROPE)
DIFF_QS = LOG2E / math.sqrt(DIFF_QK)
GQA_QS = LOG2E / math.sqrt(GQA_DIM)

LANES = 128
BF16_SUBLANES = 16
VMEM_LIMIT_BYTES = 56 * 1024 * 1024

N_SCORE_HEADS = 20
N_KEY_SLABS = 7
N_VALUE_HEADS = 10
V_ROWS = HEAD_V + BF16_SUBLANES
KEY_SLAB_OF_HEAD = (0, 1, 2, 3, 4, 4, 4, 4, 5, 5, 5, 5, 6, 6, 6, 6, 6, 6, 6, 6)
VALUE_HEAD_OF_HEAD = (0, 1, 2, 3, 4, 4, 5, 5, 6, 6, 7, 7, 8, 8, 8, 8, 9, 9, 9, 9)

TM = 256
TQ = 512
TK = 768
KEY_CHUNK = 256

WK_COLS = 1152
WQ_ROWS = 2304


def _rms_rows(x, eps=EPS):
    return lax.rsqrt(jnp.mean(x * x, axis=-1, keepdims=True) + eps)


def _rms_cols(x, eps=EPS):
    return lax.rsqrt(jnp.mean(x * x, axis=0, keepdims=True) + eps)


def _adaln_kernel(c_ref, w_ref, b_ref, o_ref):
    c = c_ref[...]
    sc = c * jax.nn.sigmoid(c)
    o_ref[0] = jnp.dot(sc.astype(BF16), w_ref[0].astype(BF16),
                       preferred_element_type=F32) + b_ref[0]


def _adaln(cvec, w_ada, b_ada):
    depth = w_ada.shape[0]
    tn = 1024
    n_out = w_ada.shape[2]
    return pl.pallas_call(
        _adaln_kernel,
        grid=(depth, n_out // tn),
        in_specs=[
            pl.BlockSpec((8, D_MODEL), lambda l, j: (0, 0)),
            pl.BlockSpec((1, D_MODEL, tn), lambda l, j: (l, 0, j)),
            pl.BlockSpec((1, 1, tn), lambda l, j: (l, 0, j)),
        ],
        out_specs=pl.BlockSpec((1, 8, tn), lambda l, j: (l, 0, j)),
        out_shape=jax.ShapeDtypeStruct((depth, 8, n_out), F32),
        compiler_params=pltpu.CompilerParams(
            dimension_semantics=("arbitrary", "arbitrary"),
            vmem_limit_bytes=VMEM_LIMIT_BYTES),
        name="adaln",
    )(cvec, w_ada, b_ada.reshape(depth, 1, n_out))


def _proj_kernel(x_ref, mod_ref, gpre_ref, wk_ref, wqT_ref, wqbT_ref, wkvk_ref, wkvvT_ref,
                 gqmla_ref, gkv_tok_ref, gkvT_ref, ggq_ref, ggqrot_ref, ggk_ref, ggkrot_ref,
                 cS_ref, sS_ref, cL_ref, sL_ref, cST_ref, sST_ref, cLT_ref, sLT_ref,
                 qT_ref, k_ref, vT_ref):
    tm = x_ref.shape[1]
    x = x_ref[0]
    mod = mod_ref[0]
    sh = mod[:, 0:D_MODEL]
    sc = mod[:, D_MODEL:2 * D_MODEL]
    h = (x * _rms_rows(x) * gpre_ref[...]) * (1.0 + sc) + sh
    hb = h.astype(BF16)
    pk = jnp.dot(hb, wk_ref[...], preferred_element_type=F32)
    pT = lax.dot_general(wqT_ref[...], hb, (((1,), (1,)), ((), ())),
                         preferred_element_type=F32)

    cS = cS_ref[...]
    sS = sS_ref[...]
    cL = cL_ref[...]
    sL = sL_ref[...]
    cST = cST_ref[...]
    sST = sST_ref[...]
    cLT = cLT_ref[...]
    sLT = sLT_ref[...]
    ones_rows = jnp.ones((BF16_SUBLANES, tm), BF16)
    zeros32 = jnp.zeros((32, tm), BF16)
    zeros64 = jnp.zeros((64, tm), BF16)

    qaT = pT[0:256]
    qn = (qaT * _rms_cols(qaT) * gqmla_ref[...]).astype(BF16)
    qm = jnp.dot(wqbT_ref[...], qn, preferred_element_type=F32)
    for hh in range(MLA_HEADS):
        blk = qm[128 * hh:128 * (hh + 1)]
        pe = blk[64:96] * cST + blk[96:128] * sST
        qT_ref[0, hh, 0:64, :] = (blk[0:64] * MLA_QS).astype(BF16)
        qT_ref[0, hh, 64:96, :] = (pe * MLA_QS).astype(BF16)
        qT_ref[0, hh, 96:128, :] = zeros32

    ckv = pk[:, 0:128]
    cn = (ckv * _rms_rows(ckv) * gkv_tok_ref[...]).astype(BF16)
    kn = jnp.dot(cn, wkvk_ref[...], preferred_element_type=F32)
    pe_tok = pk[:, 128:256] * cS[:, 0:128] + pk[:, 256:384] * sS[:, 0:128]
    for hh in range(MLA_HEADS):
        k_ref[0, :, 128 * hh:128 * (hh + 1)] = (kn[:, 128 * hh:128 * (hh + 1)] + pe_tok).astype(BF16)

    ckvT = pT[256:384]
    cnT = (ckvT * _rms_cols(ckvT) * gkvT_ref[...]).astype(BF16)
    vmT = jnp.dot(wkvvT_ref[...], cnT, preferred_element_type=F32)
    for hh in range(MLA_HEADS):
        vT_ref[0, hh, 0:64, :] = vmT[64 * hh:64 * (hh + 1)].astype(BF16)
        vT_ref[0, hh, 64:V_ROWS, :] = ones_rows

    qd = pT[384:640].reshape(8, 32, tm)
    qdr = pT[640:896].reshape(8, 32, tm)
    qd = (qd * cST[None] + qdr * sST[None]) * DIFF_QS
    for j in range(8):
        for rb in range(4):
            val = qd[j].astype(BF16) if rb == j % 4 else zeros32
            qT_ref[0, 4 + j, 32 * rb:32 * (rb + 1), :] = val
    kd = pk[:, 384:640] * cS + pk[:, 640:896] * sS
    k_ref[0, :, 512:768] = kd.astype(BF16)
    vdT = pT[1920:2176]
    for hh in range(DIFF_HEADS):
        vT_ref[0, 4 + hh, 0:64, :] = vdT[64 * hh:64 * (hh + 1)].astype(BF16)
        vT_ref[0, 4 + hh, 64:V_ROWS, :] = ones_rows

    gq = pT[896:1408].reshape(8, 64, tm)
    gqr = pT[1408:1920].reshape(8, 64, tm)
    rq = lax.rsqrt(jnp.mean(gq * gq, axis=1, keepdims=True) + EPS)
    qg = ((gq * rq * ggq_ref[...][None]) * cLT[None]
          + (gqr * rq * ggqrot_ref[...][None]) * sLT[None]) * GQA_QS
    for j in range(8):
        grp = j // 4
        for rb in range(2):
            val = qg[j].astype(BF16) if rb == grp else zeros64
            qT_ref[0, 12 + j, 64 * rb:64 * (rb + 1), :] = val
    gk = pk[:, 896:1024]
    gkr = pk[:, 1024:1152]
    sq = gk * gk
    lane = lax.broadcasted_iota(jnp.int32, sq.shape, 1)
    lo = lane < GQA_DIM
    s0 = jnp.sum(jnp.where(lo, sq, 0.0), axis=-1, keepdims=True)
    s1 = jnp.sum(jnp.where(lo, 0.0, sq), axis=-1, keepdims=True)
    rk = jnp.where(lo, lax.rsqrt(s0 / GQA_DIM + EPS), lax.rsqrt(s1 / GQA_DIM + EPS))
    kg = (gk * rk * ggk_ref[...]) * cL + (gkr * rk * ggkrot_ref[...]) * sL
    k_ref[0, :, 768:896] = kg.astype(BF16)
    gvT = pT[2176:2304]
    for hh in range(GQA_KV_HEADS):
        vT_ref[0, 8 + hh, 0:64, :] = gvT[64 * hh:64 * (hh + 1)].astype(BF16)
        vT_ref[0, 8 + hh, 64:V_ROWS, :] = ones_rows


def _proj(x_all, mod3, lw, tabs):
    bsz, t_all, _ = x_all.shape
    nt = t_all // TM
    const = lambda shape: pl.BlockSpec(shape, lambda b, t: (0,) * len(shape))
    in_specs = [
        pl.BlockSpec((1, TM, D_MODEL), lambda b, t: (b, t, 0)),
        pl.BlockSpec((1, 1, 6 * D_MODEL), lambda b, t: (jnp.where(t == nt - 1, 2, b), 0, 0)),
        const((1, D_MODEL)),
        const((D_MODEL, WK_COLS)),
        const((WQ_ROWS, D_MODEL)),
        const((512, 256)),
        const((128, 512)),
        const((256, 128)),
        const((256, TM)),
        const((1, 128)),
        const((128, TM)),
        const((64, TM)),
        const((64, TM)),
        const((1, 128)),
        const((1, 128)),
        pl.BlockSpec((TM, 256), lambda b, t: (t, 0)),
        pl.BlockSpec((TM, 256), lambda b, t: (t, 0)),
        pl.BlockSpec((TM, 128), lambda b, t: (t, 0)),
        pl.BlockSpec((TM, 128), lambda b, t: (t, 0)),
        pl.BlockSpec((32, TM), lambda b, t: (0, t)),
        pl.BlockSpec((32, TM), lambda b, t: (0, t)),
        pl.BlockSpec((64, TM), lambda b, t: (0, t)),
        pl.BlockSpec((64, TM), lambda b, t: (0, t)),
    ]
    out_specs = [
        pl.BlockSpec((1, N_SCORE_HEADS, 128, TM), lambda b, t: (b, 0, 0, t)),
        pl.BlockSpec((1, TM, N_KEY_SLABS * 128), lambda b, t: (b, t, 0)),
        pl.BlockSpec((1, N_VALUE_HEADS, V_ROWS, TM), lambda b, t: (b, 0, 0, t)),
    ]
    out_shape = [
        jax.ShapeDtypeStruct((bsz, N_SCORE_HEADS, 128, t_all), BF16),
        jax.ShapeDtypeStruct((bsz, t_all, N_KEY_SLABS * 128), BF16),
        jax.ShapeDtypeStruct((bsz, N_VALUE_HEADS, V_ROWS, t_all), BF16),
    ]
    return pl.pallas_call(
        _proj_kernel,
        grid=(bsz, nt),
        in_specs=in_specs,
        out_specs=out_specs,
        out_shape=out_shape,
        compiler_params=pltpu.CompilerParams(
            dimension_semantics=("parallel", "parallel"),
            vmem_limit_bytes=VMEM_LIMIT_BYTES),
        name="proj",
    )(x_all, mod3, lw["g_attn_pre"], lw["wk"], lw["wqT"], lw["wqbT"], lw["wkvk"], lw["wkvvT"],
      lw["gqmla_b"], lw["gkv_tok"], lw["gkvT_b"], lw["ggq_b"], lw["ggqrot_b"], lw["ggk128"],
      lw["ggkrot128"],
      tabs["cS"], tabs["sS"], tabs["cL"], tabs["sL"], tabs["cST"], tabs["sST"], tabs["cLT"],
      tabs["sLT"])


def _attn_kernel(qT_ref, k_ref, vT_ref, lam_ref, gsub_ref, yT_ref, acc_ref, m_ref, *,
                 n_k, lam_init):
    ki = pl.program_id(2)
    tq = qT_ref.shape[3]

    @pl.when(ki == 0)
    def _init():
        acc_ref[...] = jnp.zeros(acc_ref.shape, F32)
        m_ref[...] = jnp.full(m_ref.shape, -jnp.inf, F32)

    tk = k_ref.shape[1]
    n_chunks = tk // KEY_CHUNK

    def scores(h, c):
        slab = KEY_SLAB_OF_HEAD[h]
        ks = k_ref[0, KEY_CHUNK * c:KEY_CHUNK * (c + 1), 128 * slab:128 * (slab + 1)]
        return jnp.dot(ks, qT_ref[0, h], preferred_element_type=F32)

    def weighted_values(h, c, pT):
        v = vT_ref[0, VALUE_HEAD_OF_HEAD[h], :, KEY_CHUNK * c:KEY_CHUNK * (c + 1)]
        return jnp.dot(v, pT, preferred_element_type=F32)

    lookahead = 2
    pending = [[scores(h, c) for c in range(n_chunks)] for h in range(lookahead)]
    for h in range(N_SCORE_HEADS):
        s_chunks = pending.pop(0)
        m_old = m_ref[h]
        m_new = m_old
        for s in s_chunks:
            m_new = jnp.maximum(m_new, jnp.max(s, axis=0, keepdims=True))
        alpha = jnp.exp2(m_old - m_new)
        p_chunks = [jnp.exp2(s - m_new).astype(BF16) for s in s_chunks]
        nxt = []
        oT = None
        for c in range(n_chunks):
            if h + lookahead < N_SCORE_HEADS:
                nxt.append(scores(h + lookahead, c))
            o_c = weighted_values(h, c, p_chunks[c])
            oT = o_c if oT is None else oT + o_c
        if nxt:
            pending.append(nxt)
        acc_ref[h] = acc_ref[h] * alpha + oT
        m_ref[h] = m_new

    @pl.when(ki == n_k - 1)
    def _finalize():
        def head_out(h):
            a = acc_ref[h]
            return a[0:HEAD_V] / a[HEAD_V:HEAD_V + 1]
        for hh in range(MLA_HEADS):
            yT_ref[0, 64 * hh:64 * (hh + 1), :] = head_out(hh).astype(BF16)
        lp = lam_ref[...]
        l1 = jnp.sum(lp[0:1] * lp[1:2], axis=-1, keepdims=True)
        l2 = jnp.sum(lp[2:3] * lp[3:4], axis=-1, keepdims=True)
        lam = jnp.exp(l1) - jnp.exp(l2) + lam_init
        gsub = gsub_ref[...]
        for hh in range(DIFF_HEADS):
            d = head_out(4 + 2 * hh) - lam * head_out(5 + 2 * hh)
            y = (d * _rms_cols(d) * gsub) * (1.0 - lam_init)
            yT_ref[0, 256 + 64 * hh:256 + 64 * (hh + 1), :] = y.astype(BF16)
        for j in range(GQA_HEADS):
            yT_ref[0, 512 + 64 * j:512 + 64 * (j + 1), :] = head_out(12 + j).astype(BF16)


def _attn(qT, kslab, vT, lam_vecs, gsub_b, lam_init, *, tq, tk, q_blocks, q_off, k_blocks, k_off,
          name):
    bsz = qT.shape[0]
    return pl.pallas_call(
        functools.partial(_attn_kernel, n_k=k_blocks, lam_init=lam_init),
        grid=(bsz, q_blocks, k_blocks),
        in_specs=[
            pl.BlockSpec((1, N_SCORE_HEADS, 128, tq), lambda b, qi, ki: (b, 0, 0, qi + q_off)),
            pl.BlockSpec((1, tk, N_KEY_SLABS * 128), lambda b, qi, ki: (b, ki + k_off, 0)),
            pl.BlockSpec((1, N_VALUE_HEADS, V_ROWS, tk), lambda b, qi, ki: (b, 0, 0, ki + k_off)),
            pl.BlockSpec((4, DIFF_QK), lambda b, qi, ki: (0, 0)),
            pl.BlockSpec((HEAD_V, tq), lambda b, qi, ki: (0, 0)),
        ],
        out_specs=pl.BlockSpec((1, MIX_WIDTH, tq), lambda b, qi, ki: (b, 0, qi)),
        out_shape=jax.ShapeDtypeStruct((bsz, MIX_WIDTH, q_blocks * tq), BF16),
        scratch_shapes=[
            pltpu.VMEM((N_SCORE_HEADS, V_ROWS, tq), F32),
            pltpu.VMEM((N_SCORE_HEADS, 1, tq), F32),
        ],
        compiler_params=pltpu.CompilerParams(
            dimension_semantics=("parallel", "parallel", "arbitrary"),
            vmem_limit_bytes=VMEM_LIMIT_BYTES),
        name=name,
    )(qT, kslab, vT, lam_vecs, gsub_b)


def _out_ffn_kernel(*refs, n_lat_tiles, has_ctx):
    if has_ctx:
        yT_ref, yTc_ref = refs[:2]
        refs = refs[2:]
        yT = jnp.where(pl.program_id(1) == n_lat_tiles, yTc_ref[0], yT_ref[0])
    else:
        yT = refs[0][0]
        refs = refs[1:]
    x_ref, mod_ref, gpost_ref, gfpre_ref, gfpost_ref, wout_ref, wg_ref, wu_ref, wd_ref, o_ref = refs
    yp = lax.dot_general(yT, wout_ref[...], (((0,), (0,)), ((), ())),
                         preferred_element_type=F32)
    mod = mod_ref[0]
    gt_a = mod[:, 2 * D_MODEL:3 * D_MODEL]
    sh_f = mod[:, 3 * D_MODEL:4 * D_MODEL]
    sc_f = mod[:, 4 * D_MODEL:5 * D_MODEL]
    gt_f = mod[:, 5 * D_MODEL:6 * D_MODEL]
    x1 = x_ref[0] + gt_a * (yp * _rms_rows(yp) * gpost_ref[...])
    hf = (x1 * _rms_rows(x1) * gfpre_ref[...]) * (1.0 + sc_f) + sh_f
    hb = hf.astype(BF16)
    g = jnp.dot(hb, wg_ref[...], preferred_element_type=F32)
    u = jnp.dot(hb, wu_ref[...], preferred_element_type=F32)
    a = (g * jax.nn.sigmoid(g)) * u
    f = jnp.dot(a.astype(BF16), wd_ref[...], preferred_element_type=F32)
    o_ref[0] = x1 + gt_f * (f * _rms_rows(f) * gfpost_ref[...])


def _out_ffn(yT, yT_ctx, x_all, mod3, lw):
    bsz = x_all.shape[0]
    nt_all = x_all.shape[1] // TM
    n_lat = SEQ // TM
    has_ctx = yT_ctx is not None
    n_tiles = n_lat + 1 if has_ctx else n_lat
    const = lambda shape: pl.BlockSpec(shape, lambda b, t: (0,) * len(shape),
                                       pipeline_mode=pl.Buffered(1))
    y_specs = [pl.BlockSpec((1, MIX_WIDTH, TM), lambda b, t: (b, 0, jnp.minimum(t, n_lat - 1)))]
    y_args = [yT]
    if has_ctx:
        y_specs.append(pl.BlockSpec((1, MIX_WIDTH, TM), lambda b, t: (b, 0, 0)))
        y_args.append(yT_ctx)
    return pl.pallas_call(
        functools.partial(_out_ffn_kernel, n_lat_tiles=n_lat, has_ctx=has_ctx),
        grid=(bsz, n_tiles),
        in_specs=y_specs + [
            pl.BlockSpec((1, TM, D_MODEL), lambda b, t: (b, t, 0)),
            pl.BlockSpec((1, 1, 6 * D_MODEL), lambda b, t: (jnp.where(t == nt_all - 1, 2, b), 0, 0)),
            const((1, D_MODEL)),
            const((1, D_MODEL)),
            const((1, D_MODEL)),
            const((MIX_WIDTH, D_MODEL)),
            const((D_MODEL, FFN_HIDDEN)),
            const((D_MODEL, FFN_HIDDEN)),
            const((FFN_HIDDEN, D_MODEL)),
        ],
        out_specs=pl.BlockSpec((1, TM, D_MODEL), lambda b, t: (b, t, 0)),
        out_shape=jax.ShapeDtypeStruct((bsz, n_tiles * TM, D_MODEL), F32),
        compiler_params=pltpu.CompilerParams(
            dimension_semantics=("parallel", "parallel"),
            vmem_limit_bytes=VMEM_LIMIT_BYTES),
        name="out_ffn",
    )(*y_args, x_all, mod3, lw["g_attn_post"], lw["g_ffn_pre"], lw["g_ffn_post"],
      lw["w_out"], lw["w_gate"], lw["w_up"], lw["w_down"])


def _rot_cols(w, half):
    n = w.shape[-1]
    wg = w.reshape(w.shape[:-1] + (n // (2 * half), 2, half))
    return jnp.concatenate([-wg[..., 1:2, :], wg[..., 0:1, :]], axis=-2).reshape(w.shape)


def _swap_halves(g, half):
    return jnp.concatenate([g[half:], g[:half]])


def _rope_tables():
    t = jnp.arange(SEQ, dtype=jnp.int32)
    row = (t // GRID_W).astype(F32)
    col = (t % GRID_W).astype(F32)

    def tables(rot_dim):
        quarter = rot_dim // 4
        inv = ROPE_THETA ** (-jnp.arange(quarter, dtype=F32) / quarter)
        ang = jnp.concatenate([row[:, None] * inv, col[:, None] * inv], axis=-1)
        cos = jnp.concatenate([jnp.cos(ang), jnp.ones((CTX_LEN, rot_dim // 2), F32)], axis=0)
        sin = jnp.concatenate([jnp.sin(ang), jnp.zeros((CTX_LEN, rot_dim // 2), F32)], axis=0)
        return jnp.tile(cos, (1, 2)), jnp.tile(sin, (1, 2))

    c_s, s_s = tables(MLA_ROPE)
    c_l, s_l = tables(GQA_DIM)
    return {
        "cS": jnp.tile(c_s, (1, 8)), "sS": jnp.tile(s_s, (1, 8)),
        "cL": jnp.tile(c_l, (1, 2)), "sL": jnp.tile(s_l, (1, 2)),
        "cST": c_s.T, "sST": s_s.T,
        "cLT": c_l.T, "sLT": s_l.T,
    }


def _layer_weights(l, w_in, g_attn_pre, g_attn_post, g_mla_q, w_mla_qb, g_mla_kv, w_mla_kvb,
                   g_diff_sub, g_gqa_q, g_gqa_k, w_out, g_ffn_pre, g_ffn_post, w_ffn_gate,
                   w_ffn_up, w_ffn_down, lambdas):
    wi = w_in[l]
    q_a = wi[:, 0:256]
    c_kv = wi[:, 256:384]
    k_pe = wi[:, 384:416]
    dq = wi[:, 416:672]
    dk = wi[:, 672:928]
    dv = wi[:, 928:1184]
    gq = wi[:, 1184:1696]
    gk = wi[:, 1696:1824]
    gv = wi[:, 1824:1952]
    zeros = lambda n: jnp.zeros((D_MODEL, n), F32)
    pe_slab = jnp.concatenate([zeros(64), k_pe, zeros(32)], axis=1)
    pe_rot_slab = jnp.concatenate([zeros(64), _rot_cols(k_pe, 16), zeros(32)], axis=1)
    wk = jnp.concatenate([c_kv, pe_slab, pe_rot_slab, dk, _rot_cols(dk, 16), gk, _rot_cols(gk, 32)],
                         axis=1)
    wq = jnp.concatenate([q_a, c_kv, dq, _rot_cols(dq, 16), gq, _rot_cols(gq, 32), dv, gv], axis=1)
    qb = w_mla_qb[l].reshape(MLA_Q_RANK, MLA_HEADS, MLA_NOPE + MLA_ROPE)
    qb_pe = qb[:, :, MLA_NOPE:]
    wqb = jnp.concatenate([qb, _rot_cols(qb_pe, 16)], axis=-1).reshape(MLA_Q_RANK, 512)
    kvb = w_mla_kvb[l].reshape(MLA_KV_RANK, MLA_HEADS, MLA_NOPE + MLA_V)
    wkvk = jnp.concatenate([kvb[:, :, :MLA_NOPE], jnp.zeros((MLA_KV_RANK, MLA_HEADS, 64), F32)],
                           axis=-1).reshape(MLA_KV_RANK, 512)
    wkvv = kvb[:, :, MLA_NOPE:].reshape(MLA_KV_RANK, MLA_HEADS * MLA_V)
    bcast = lambda g, n: jnp.broadcast_to(g[:, None], (g.shape[0], n))
    return {
        "g_attn_pre": g_attn_pre[l][None], "g_attn_post": g_attn_post[l][None],
        "g_ffn_pre": g_ffn_pre[l][None], "g_ffn_post": g_ffn_post[l][None],
        "wk": wk.astype(BF16), "wqT": wq.T.astype(BF16), "wqbT": wqb.T.astype(BF16),
        "wkvk": wkvk.astype(BF16), "wkvvT": wkvv.T.astype(BF16),
        "gqmla_b": bcast(g_mla_q[l], TM), "gkv_tok": g_mla_kv[l][None],
        "gkvT_b": bcast(g_mla_kv[l], TM),
        "ggq_b": bcast(g_gqa_q[l], TM), "ggqrot_b": bcast(_swap_halves(g_gqa_q[l], 32), TM),
        "ggk128": jnp.tile(g_gqa_k[l], 2)[None],
        "ggkrot128": jnp.tile(_swap_halves(g_gqa_k[l], 32), 2)[None],
        "gsub_b": bcast(g_diff_sub[l], TQ),
        "lam_vecs": jnp.stack([lam[l] for lam in lambdas]),
        "w_out": w_out[l].astype(BF16), "w_gate": w_ffn_gate[l].astype(BF16),
        "w_up": w_ffn_up[l].astype(BF16), "w_down": w_ffn_down[l].astype(BF16),
    }


def kernel(x, c, ctx, c_ctx, w_ada, b_ada, g_attn_pre, g_attn_post, w_in, g_mla_q, w_mla_qb, g_mla_kv, w_mla_kvb, lambda_q1, lambda_k1, lambda_q2, lambda_k2, g_diff_sub, g_gqa_q, g_gqa_k, w_out, g_ffn_pre, g_ffn_post, w_ffn_gate, w_ffn_up, w_ffn_down):
    bsz = x.shape[0]
    depth = w_ada.shape[0]
    assert x.shape == (bsz, SEQ, D_MODEL) and ctx.shape == (bsz, CTX_LEN, D_MODEL) and bsz == 2
    tabs = _rope_tables()
    cvec = jnp.concatenate([c, c_ctx[None], jnp.zeros((8 - bsz - 1, D_MODEL), F32)], axis=0)
    mod_all = _adaln(cvec, w_ada, b_ada)
    x_all = jnp.concatenate([x, ctx], axis=1)
    for l in range(depth):
        last = l == depth - 1
        lam_init = 0.8 - 0.6 * math.exp(-0.3 * l)
        lw = _layer_weights(l, w_in, g_attn_pre, g_attn_post, g_mla_q, w_mla_qb, g_mla_kv,
                            w_mla_kvb, g_diff_sub, g_gqa_q, g_gqa_k, w_out, g_ffn_pre, g_ffn_post,
                            w_ffn_gate, w_ffn_up, w_ffn_down,
                            (lambda_q1, lambda_k1, lambda_q2, lambda_k2))
        mod3 = mod_all[l].reshape(8, 1, 6 * D_MODEL)
        qT, kslab, vT = _proj(x_all, mod3, lw, tabs)
        yT = _attn(qT, kslab, vT, lw["lam_vecs"], lw["gsub_b"], lam_init, tq=TQ, tk=TK,
                   q_blocks=SEQ // TQ, q_off=0, k_blocks=T_ALL // TK, k_off=0, name="attn")
        yT_ctx = None
        if not last:
            yT_ctx = _attn(qT, kslab, vT, lw["lam_vecs"], lw["gsub_b"][:, :CTX_LEN], lam_init,
                           tq=CTX_LEN, tk=CTX_LEN, q_blocks=1, q_off=SEQ // CTX_LEN, k_blocks=1,
                           k_off=SEQ // CTX_LEN, name="attn_ctx")
        x_all = _out_ffn(yT, yT_ctx, x_all, mod3, lw)
    return x_all
```

```python
import functools
import math

import jax
import jax.numpy as jnp
from jax import lax
from jax.experimental import pallas as pl
from jax.experimental.pallas import tpu as pltpu

F32 = jnp.float32
BF16 = jnp.bfloat16

D_MODEL = 1024
SEQ = 8192
GRID_W = 64
CTX_LEN = 256
T_ALL = SEQ + CTX_LEN
ROPE_THETA = 10000.0
EPS = 1e-6

MLA_HEADS = 4
MLA_Q_RANK = 256
MLA_KV_RANK = 128
MLA_NOPE = 64
MLA_ROPE = 32
MLA_V = 64
DIFF_HEADS = 4
DIFF_QK = 32
DIFF_V = 64
GQA_HEADS = 8
GQA_KV_HEADS = 2
GQA_DIM = 64
HEAD_V = 64
MIX_WIDTH = 1024
FFN_HIDDEN = 2816

LOG2E = 1.4426950408889634
MLA_QS = LOG2E / math.sqrt(MLA_NOPE + MLA_ROPE)
DIFF_QS = LOG2E / math.sqrt(DIFF_QK)
GQA_QS = LOG2E / math.sqrt(GQA_DIM)

LANES = 128
BF16_SUBLANES = 16
VMEM_LIMIT_BYTES = 56 * 1024 * 1024

N_SCORE_HEADS = 20
N_KEY_SLABS = 7
N_VALUE_HEADS = 10
V_ROWS = HEAD_V + BF16_SUBLANES
KEY_SLAB_OF_HEAD = (0, 1, 2, 3, 4, 4, 4, 4, 5, 5, 5, 5, 6, 6, 6, 6, 6, 6, 6, 6)
VALUE_HEAD_OF_HEAD = (0, 1, 2, 3, 4, 4, 5, 5, 6, 6, 7, 7, 8, 8, 8, 8, 9, 9, 9, 9)

TM = 256
TQ = 512
TK = 768
KEY_CHUNK = 256
STALE_MAX_SLACK = 64.0

WK_COLS = 1152
WQ_ROWS = 2304


def _rms_rows(x, eps=EPS):
    return lax.rsqrt(jnp.mean(x * x, axis=-1, keepdims=True) + eps)


def _rms_cols(x, eps=EPS):
    return lax.rsqrt(jnp.mean(x * x, axis=0, keepdims=True) + eps)


def _adaln_kernel(c_ref, w_ref, b_ref, o_ref):
    c = c_ref[...]
    sc = c * jax.nn.sigmoid(c)
    o_ref[0] = jnp.dot(sc.astype(BF16), w_ref[0].astype(BF16),
                       preferred_element_type=F32) + b_ref[0]


def _adaln(cvec, w_ada, b_ada):
    depth = w_ada.shape[0]
    tn = 1024
    n_out = w_ada.shape[2]
    return pl.pallas_call(
        _adaln_kernel,
        grid=(depth, n_out // tn),
        in_specs=[
            pl.BlockSpec((8, D_MODEL), lambda l, j: (0, 0)),
            pl.BlockSpec((1, D_MODEL, tn), lambda l, j: (l, 0, j)),
            pl.BlockSpec((1, 1, tn), lambda l, j: (l, 0, j)),
        ],
        out_specs=pl.BlockSpec((1, 8, tn), lambda l, j: (l, 0, j)),
        out_shape=jax.ShapeDtypeStruct((depth, 8, n_out), F32),
        compiler_params=pltpu.CompilerParams(
            dimension_semantics=("arbitrary", "arbitrary"),
            vmem_limit_bytes=VMEM_LIMIT_BYTES),
        name="adaln",
    )(cvec, w_ada, b_ada.reshape(depth, 1, n_out))


def _proj_kernel(x_ref, mod_ref, gpre_ref, wk_ref, wqT_ref, wqbT_ref, wkvk_ref, wkvvT_ref,
                 gqmla_ref, gkv_tok_ref, gkvT_ref, ggq_ref, ggqrot_ref, ggk_ref, ggkrot_ref,
                 cS_ref, sS_ref, cL_ref, sL_ref, cST_ref, sST_ref, cLT_ref, sLT_ref,
                 qT_ref, k_ref, vT_ref):
    tm = x_ref.shape[1]
    x = x_ref[0]
    mod = mod_ref[0]
    sh = mod[:, 0:D_MODEL]
    sc = mod[:, D_MODEL:2 * D_MODEL]
    h = (x * _rms_rows(x) * gpre_ref[...]) * (1.0 + sc) + sh
    hb = h.astype(BF16)
    pk = jnp.dot(hb, wk_ref[...], preferred_element_type=F32)
    pT = lax.dot_general(wqT_ref[...], hb, (((1,), (1,)), ((), ())),
                         preferred_element_type=F32)

    cS = cS_ref[...]
    sS = sS_ref[...]
    cL = cL_ref[...]
    sL = sL_ref[...]
    cST = cST_ref[...]
    sST = sST_ref[...]
    cLT = cLT_ref[...]
    sLT = sLT_ref[...]
    ones_rows = jnp.ones((BF16_SUBLANES, tm), BF16)
    zeros32 = jnp.zeros((32, tm), BF16)
    zeros64 = jnp.zeros((64, tm), BF16)

    qaT = pT[0:256]
    qn = (qaT * _rms_cols(qaT) * gqmla_ref[...]).astype(BF16)
    qm = jnp.dot(wqbT_ref[...], qn, preferred_element_type=F32)
    for hh in range(MLA_HEADS):
        blk = qm[128 * hh:128 * (hh + 1)]
        pe = blk[64:96] * cST + blk[96:128] * sST
        qT_ref[0, hh, 0:64, :] = (blk[0:64] * MLA_QS).astype(BF16)
        qT_ref[0, hh, 64:96, :] = (pe * MLA_QS).astype(BF16)
        qT_ref[0, hh, 96:128, :] = zeros32

    ckv = pk[:, 0:128]
    cn = (ckv * _rms_rows(ckv) * gkv_tok_ref[...]).astype(BF16)
    kn = jnp.dot(cn, wkvk_ref[...], preferred_element_type=F32)
    pe_tok = pk[:, 128:256] * cS[:, 0:128] + pk[:, 256:384] * sS[:, 0:128]
    for hh in range(MLA_HEADS):
        k_ref[0, :, 128 * hh:128 * (hh + 1)] = (kn[:, 128 * hh:128 * (hh + 1)] + pe_tok).astype(BF16)

    ckvT = pT[256:384]
    cnT = (ckvT * _rms_cols(ckvT) * gkvT_ref[...]).astype(BF16)
    vmT = jnp.dot(wkvvT_ref[...], cnT, preferred_element_type=F32)
    for hh in range(MLA_HEADS):
        vT_ref[0, hh, 0:64, :] = vmT[64 * hh:64 * (hh + 1)].astype(BF16)
        vT_ref[0, hh, 64:V_ROWS, :] = ones_rows

    qd = pT[384:640].reshape(8, 32, tm)
    qdr = pT[640:896].reshape(8, 32, tm)
    qd = (qd * cST[None] + qdr * sST[None]) * DIFF_QS
    for j in range(8):
        for rb in range(4):
            val = qd[j].astype(BF16) if rb == j % 4 else zeros32
            qT_ref[0, 4 + j, 32 * rb:32 * (rb + 1), :] = val
    kd = pk[:, 384:640] * cS + pk[:, 640:896] * sS
    k_ref[0, :, 512:768] = kd.astype(BF16)
    vdT = pT[1920:2176]
    for hh in range(DIFF_HEADS):
        vT_ref[0, 4 + hh, 0:64, :] = vdT[64 * hh:64 * (hh + 1)].astype(BF16)
        vT_ref[0, 4 + hh, 64:V_ROWS, :] = ones_rows

    gq = pT[896:1408].reshape(8, 64, tm)
    gqr = pT[1408:1920].reshape(8, 64, tm)
    rq = lax.rsqrt(jnp.mean(gq * gq, axis=1, keepdims=True) + EPS)
    qg = ((gq * rq * ggq_ref[...][None]) * cLT[None]
          + (gqr * rq * ggqrot_ref[...][None]) * sLT[None]) * GQA_QS
    for j in range(8):
        grp = j // 4
        for rb in range(2):
            val = qg[j].astype(BF16) if rb == grp else zeros64
            qT_ref[0, 12 + j, 64 * rb:64 * (rb + 1), :] = val
    gk = pk[:, 896:1024]
    gkr = pk[:, 1024:1152]
    sq = gk * gk
    lane = lax.broadcasted_iota(jnp.int32, sq.shape, 1)
    lo = lane < GQA_DIM
    s0 = jnp.sum(jnp.where(lo, sq, 0.0), axis=-1, keepdims=True)
    s1 = jnp.sum(jnp.where(lo, 0.0, sq), axis=-1, keepdims=True)
    rk = jnp.where(lo, lax.rsqrt(s0 / GQA_DIM + EPS), lax.rsqrt(s1 / GQA_DIM + EPS))
    kg = (gk * rk * ggk_ref[...]) * cL + (gkr * rk * ggkrot_ref[...]) * sL
    k_ref[0, :, 768:896] = kg.astype(BF16)
    gvT = pT[2176:2304]
    for hh in range(GQA_KV_HEADS):
        vT_ref[0, 8 + hh, 0:64, :] = gvT[64 * hh:64 * (hh + 1)].astype(BF16)
        vT_ref[0, 8 + hh, 64:V_ROWS, :] = ones_rows


def _proj(x_all, mod3, lw, tabs):
    bsz, t_all, _ = x_all.shape
    nt = t_all // TM
    const = lambda shape: pl.BlockSpec(shape, lambda b, t: (0,) * len(shape))
    in_specs = [
        pl.BlockSpec((1, TM, D_MODEL), lambda b, t: (b, t, 0)),
        pl.BlockSpec((1, 1, 6 * D_MODEL), lambda b, t: (jnp.where(t == nt - 1, 2, b), 0, 0)),
        const((1, D_MODEL)),
        const((D_MODEL, WK_COLS)),
        const((WQ_ROWS, D_MODEL)),
        const((512, 256)),
        const((128, 512)),
        const((256, 128)),
        const((256, TM)),
        const((1, 128)),
        const((128, TM)),
        const((64, TM)),
        const((64, TM)),
        const((1, 128)),
        const((1, 128)),
        pl.BlockSpec((TM, 256), lambda b, t: (t, 0)),
        pl.BlockSpec((TM, 256), lambda b, t: (t, 0)),
        pl.BlockSpec((TM, 128), lambda b, t: (t, 0)),
        pl.BlockSpec((TM, 128), lambda b, t: (t, 0)),
        pl.BlockSpec((32, TM), lambda b, t: (0, t)),
        pl.BlockSpec((32, TM), lambda b, t: (0, t)),
        pl.BlockSpec((64, TM), lambda b, t: (0, t)),
        pl.BlockSpec((64, TM), lambda b, t: (0, t)),
    ]
    out_specs = [
        pl.BlockSpec((1, N_SCORE_HEADS, 128, TM), lambda b, t: (b, 0, 0, t)),
        pl.BlockSpec((1, TM, N_KEY_SLABS * 128), lambda b, t: (b, t, 0)),
        pl.BlockSpec((1, N_VALUE_HEADS, V_ROWS, TM), lambda b, t: (b, 0, 0, t)),
    ]
    out_shape = [
        jax.ShapeDtypeStruct((bsz, N_SCORE_HEADS, 128, t_all), BF16),
        jax.ShapeDtypeStruct((bsz, t_all, N_KEY_SLABS * 128), BF16),
        jax.ShapeDtypeStruct((bsz, N_VALUE_HEADS, V_ROWS, t_all), BF16),
    ]
    return pl.pallas_call(
        _proj_kernel,
        grid=(bsz, nt),
        in_specs=in_specs,
        out_specs=out_specs,
        out_shape=out_shape,
        compiler_params=pltpu.CompilerParams(
            dimension_semantics=("parallel", "parallel"),
            vmem_limit_bytes=VMEM_LIMIT_BYTES),
        name="proj",
    )(x_all, mod3, lw["g_attn_pre"], lw["wk"], lw["wqT"], lw["wqbT"], lw["wkvk"], lw["wkvvT"],
      lw["gqmla_b"], lw["gkv_tok"], lw["gkvT_b"], lw["ggq_b"], lw["ggqrot_b"], lw["ggk128"],
      lw["ggkrot128"],
      tabs["cS"], tabs["sS"], tabs["cL"], tabs["sL"], tabs["cST"], tabs["sST"], tabs["cLT"],
      tabs["sLT"])


def _attn_kernel(qT_ref, k_ref, vT_ref, lam_ref, gsub_ref, yT_ref, acc_ref, m_ref, kept_ref, *,
                 n_k, lam_init):
    ki = pl.program_id(2)
    tk = k_ref.shape[1]
    n_chunks = tk // KEY_CHUNK

    def scores(h, c):
        slab = KEY_SLAB_OF_HEAD[h]
        ks = k_ref[0, KEY_CHUNK * c:KEY_CHUNK * (c + 1), 128 * slab:128 * (slab + 1)]
        return jnp.dot(ks, qT_ref[0, h], preferred_element_type=F32)

    def weighted_values(h, c, pT):
        v = vT_ref[0, VALUE_HEAD_OF_HEAD[h], :, KEY_CHUNK * c:KEY_CHUNK * (c + 1)]
        return jnp.dot(v, pT, preferred_element_type=F32)

    def column_max(m, s_chunks):
        for s in s_chunks:
            m = jnp.maximum(m, jnp.max(s, axis=0, keepdims=True))
        return m

    def sweep_heads(lookahead, head_step):
        pending = [[scores(h, c) for c in range(n_chunks)] for h in range(lookahead)]
        for h in range(N_SCORE_HEADS):
            s_chunks = pending.pop(0)
            issued = []

            def issue_ahead(c, h=h, issued=issued):
                if h + lookahead < N_SCORE_HEADS:
                    issued.append(scores(h + lookahead, c))

            head_step(h, s_chunks, issue_ahead)
            if issued:
                pending.append(issued)

    @pl.when(ki == 0)
    def _first_tile():
        def head_step(h, s_chunks, issue_ahead):
            m_new = column_max(jnp.full((1, s_chunks[0].shape[1]), -jnp.inf, F32), s_chunks)
            oT = None
            for c, s in enumerate(s_chunks):
                issue_ahead(c)
                o_c = weighted_values(h, c, jnp.exp2(s - m_new).astype(BF16))
                oT = o_c if oT is None else oT + o_c
            acc_ref[h] = oT
            m_ref[h] = m_new

        sweep_heads(2, head_step)

    if n_k > 1:
        @pl.when(ki > 0)
        def _later_tiles():
            excess = []

            def head_step(h, s_chunks, issue_ahead):
                m_used = m_ref[h]
                m_tile = m_used
                oT = None
                for c, s in enumerate(s_chunks):
                    issue_ahead(c)
                    m_tile = jnp.maximum(m_tile, jnp.max(s, axis=0, keepdims=True))
                    o_c = weighted_values(h, c, jnp.exp2(s - m_used).astype(BF16))
                    oT = o_c if oT is None else oT + o_c
                acc_old = acc_ref[h]
                over = m_tile - m_used
                keep_old = over > STALE_MAX_SLACK
                acc_ref[h] = jnp.where(keep_old, acc_old, (acc_old + oT) * jnp.exp2(-over))
                m_ref[h] = jnp.where(keep_old, m_used, m_tile)
                kept_ref[h] = jnp.where(keep_old, 1.0, 0.0)
                excess.append(over)

            sweep_heads(1, head_step)
            worst = excess[0]
            for e in excess[1:]:
                worst = jnp.maximum(worst, e)

            @pl.when(jnp.max(worst) > STALE_MAX_SLACK)
            def _redo_kept_columns():
                def redo_head(h, carry):
                    slab = jnp.where(h < 4, h, jnp.where(h < 12, 4 + ((h - 4) >> 2), 6))
                    vh = jnp.where(h < 4, h, jnp.where(h < 12, 4 + ((h - 4) >> 1),
                                                       8 + ((h - 12) >> 2)))
                    ks = k_ref[0, :, pl.ds(pl.multiple_of(slab * 128, 128), 128)]
                    sT = jnp.dot(ks, qT_ref[0, h], preferred_element_type=F32)
                    m_old = m_ref[h]
                    m_new = jnp.maximum(m_old, jnp.max(sT, axis=0, keepdims=True))
                    kept = kept_ref[h] > 0.5
                    oT = jnp.dot(vT_ref[0, vh], jnp.exp2(sT - m_new).astype(BF16),
                                 preferred_element_type=F32)
                    acc_old = acc_ref[h]
                    acc_ref[h] = jnp.where(kept, acc_old * jnp.exp2(m_old - m_new) + oT, acc_old)
                    m_ref[h] = jnp.where(kept, m_new, m_old)
                    return carry

                lax.fori_loop(0, N_SCORE_HEADS, redo_head, 0)

    @pl.when(ki == n_k - 1)
    def _finalize():
        def head_out(h):
            a = acc_ref[h]
            return a[0:HEAD_V] / a[HEAD_V:HEAD_V + 1]
        for hh in range(MLA_HEADS):
            yT_ref[0, 64 * hh:64 * (hh + 1), :] = head_out(hh).astype(BF16)
        lp = lam_ref[...]
        l1 = jnp.sum(lp[0:1] * lp[1:2], axis=-1, keepdims=True)
        l2 = jnp.sum(lp[2:3] * lp[3:4], axis=-1, keepdims=True)
        lam = jnp.exp(l1) - jnp.exp(l2) + lam_init
        gsub = gsub_ref[...]
        for hh in range(DIFF_HEADS):
            d = head_out(4 + 2 * hh) - lam * head_out(5 + 2 * hh)
            y = (d * _rms_cols(d) * gsub) * (1.0 - lam_init)
            yT_ref[0, 256 + 64 * hh:256 + 64 * (hh + 1), :] = y.astype(BF16)
        for j in range(GQA_HEADS):
            yT_ref[0, 512 + 64 * j:512 + 64 * (j + 1), :] = head_out(12 + j).astype(BF16)


def _attn(qT, kslab, vT, lam_vecs, gsub_b, lam_init, *, tq, tk, q_blocks, q_off, k_blocks, k_off,
          name):
    bsz = qT.shape[0]
    return pl.pallas_call(
        functools.partial(_attn_kernel, n_k=k_blocks, lam_init=lam_init),
        grid=(bsz, q_blocks, k_blocks),
        in_specs=[
            pl.BlockSpec((1, N_SCORE_HEADS, 128, tq), lambda b, qi, ki: (b, 0, 0, qi + q_off)),
            pl.BlockSpec((1, tk, N_KEY_SLABS * 128), lambda b, qi, ki: (b, ki + k_off, 0)),
            pl.BlockSpec((1, N_VALUE_HEADS, V_ROWS, tk), lambda b, qi, ki: (b, 0, 0, ki + k_off)),
            pl.BlockSpec((4, DIFF_QK), lambda b, qi, ki: (0, 0)),
            pl.BlockSpec((HEAD_V, tq), lambda b, qi, ki: (0, 0)),
        ],
        out_specs=pl.BlockSpec((1, MIX_WIDTH, tq), lambda b, qi, ki: (b, 0, qi)),
        out_shape=jax.ShapeDtypeStruct((bsz, MIX_WIDTH, q_blocks * tq), BF16),
        scratch_shapes=[
            pltpu.VMEM((N_SCORE_HEADS, V_ROWS, tq), F32),
            pltpu.VMEM((N_SCORE_HEADS, 1, tq), F32),
            pltpu.VMEM((N_SCORE_HEADS, 1, tq), F32),
        ],
        compiler_params=pltpu.CompilerParams(
            dimension_semantics=("parallel", "parallel", "arbitrary"),
            vmem_limit_bytes=VMEM_LIMIT_BYTES),
        name=name,
    )(qT, kslab, vT, lam_vecs, gsub_b)


def _out_ffn_kernel(*refs, n_lat_tiles, has_ctx):
    if has_ctx:
        yT_ref, yTc_ref = refs[:2]
        refs = refs[2:]
        yT = jnp.where(pl.program_id(1) == n_lat_tiles, yTc_ref[0], yT_ref[0])
    else:
        yT = refs[0][0]
        refs = refs[1:]
    x_ref, mod_ref, gpost_ref, gfpre_ref, gfpost_ref, wout_ref, wg_ref, wu_ref, wd_ref, o_ref = refs
    yp = lax.dot_general(yT, wout_ref[...], (((0,), (0,)), ((), ())),
                         preferred_element_type=F32)
    mod = mod_ref[0]
    gt_a = mod[:, 2 * D_MODEL:3 * D_MODEL]
    sh_f = mod[:, 3 * D_MODEL:4 * D_MODEL]
    sc_f = mod[:, 4 * D_MODEL:5 * D_MODEL]
    gt_f = mod[:, 5 * D_MODEL:6 * D_MODEL]
    x1 = x_ref[0] + gt_a * (yp * _rms_rows(yp) * gpost_ref[...])
    hf = (x1 * _rms_rows(x1) * gfpre_ref[...]) * (1.0 + sc_f) + sh_f
    hb = hf.astype(BF16)
    g = jnp.dot(hb, wg_ref[...], preferred_element_type=F32)
    u = jnp.dot(hb, wu_ref[...], preferred_element_type=F32)
    a = (g * jax.nn.sigmoid(g)) * u
    f = jnp.dot(a.astype(BF16), wd_ref[...], preferred_element_type=F32)
    o_ref[0] = x1 + gt_f * (f * _rms_rows(f) * gfpost_ref[...])


def _out_ffn(yT, yT_ctx, x_all, mod3, lw):
    bsz = x_all.shape[0]
    nt_all = x_all.shape[1] // TM
    n_lat = SEQ // TM
    has_ctx = yT_ctx is not None
    n_tiles = n_lat + 1 if has_ctx else n_lat
    const = lambda shape: pl.BlockSpec(shape, lambda b, t: (0,) * len(shape),
                                       pipeline_mode=pl.Buffered(1))
    y_specs = [pl.BlockSpec((1, MIX_WIDTH, TM), lambda b, t: (b, 0, jnp.minimum(t, n_lat - 1)))]
    y_args = [yT]
    if has_ctx:
        y_specs.append(pl.BlockSpec((1, MIX_WIDTH, TM), lambda b, t: (b, 0, 0)))
        y_args.append(yT_ctx)
    return pl.pallas_call(
        functools.partial(_out_ffn_kernel, n_lat_tiles=n_lat, has_ctx=has_ctx),
        grid=(bsz, n_tiles),
        in_specs=y_specs + [
            pl.BlockSpec((1, TM, D_MODEL), lambda b, t: (b, t, 0)),
            pl.BlockSpec((1, 1, 6 * D_MODEL), lambda b, t: (jnp.where(t == nt_all - 1, 2, b), 0, 0)),
            const((1, D_MODEL)),
            const((1, D_MODEL)),
            const((1, D_MODEL)),
            const((MIX_WIDTH, D_MODEL)),
            const((D_MODEL, FFN_HIDDEN)),
            const((D_MODEL, FFN_HIDDEN)),
            const((FFN_HIDDEN, D_MODEL)),
        ],
        out_specs=pl.BlockSpec((1, TM, D_MODEL), lambda b, t: (b, t, 0)),
        out_shape=jax.ShapeDtypeStruct((bsz, n_tiles * TM, D_MODEL), F32),
        compiler_params=pltpu.CompilerParams(
            dimension_semantics=("parallel", "parallel"),
            vmem_limit_bytes=VMEM_LIMIT_BYTES),
        name="out_ffn",
    )(*y_args, x_all, mod3, lw["g_attn_post"], lw["g_ffn_pre"], lw["g_ffn_post"],
      lw["w_out"], lw["w_gate"], lw["w_up"], lw["w_down"])


def _rot_cols(w, half):
    n = w.shape[-1]
    wg = w.reshape(w.shape[:-1] + (n // (2 * half), 2, half))
    return jnp.concatenate([-wg[..., 1:2, :], wg[..., 0:1, :]], axis=-2).reshape(w.shape)


def _swap_halves(g, half):
    return jnp.concatenate([g[half:], g[:half]])


def _rope_tables():
    t = jnp.arange(SEQ, dtype=jnp.int32)
    row = (t // GRID_W).astype(F32)
    col = (t % GRID_W).astype(F32)

    def tables(rot_dim):
        quarter = rot_dim // 4
        inv = ROPE_THETA ** (-jnp.arange(quarter, dtype=F32) / quarter)
        ang = jnp.concatenate([row[:, None] * inv, col[:, None] * inv], axis=-1)
        cos = jnp.concatenate([jnp.cos(ang), jnp.ones((CTX_LEN, rot_dim // 2), F32)], axis=0)
        sin = jnp.concatenate([jnp.sin(ang), jnp.zeros((CTX_LEN, rot_dim // 2), F32)], axis=0)
        return jnp.tile(cos, (1, 2)), jnp.tile(sin, (1, 2))

    c_s, s_s = tables(MLA_ROPE)
    c_l, s_l = tables(GQA_DIM)
    return {
        "cS": jnp.tile(c_s, (1, 8)), "sS": jnp.tile(s_s, (1, 8)),
        "cL": jnp.tile(c_l, (1, 2)), "sL": jnp.tile(s_l, (1, 2)),
        "cST": c_s.T, "sST": s_s.T,
        "cLT": c_l.T, "sLT": s_l.T,
    }


def _layer_weights(l, w_in, g_attn_pre, g_attn_post, g_mla_q, w_mla_qb, g_mla_kv, w_mla_kvb,
                   g_diff_sub, g_gqa_q, g_gqa_k, w_out, g_ffn_pre, g_ffn_post, w_ffn_gate,
                   w_ffn_up, w_ffn_down, lambdas):
    wi = w_in[l]
    q_a = wi[:, 0:256]
    c_kv = wi[:, 256:384]
    k_pe = wi[:, 384:416]
    dq = wi[:, 416:672]
    dk = wi[:, 672:928]
    dv = wi[:, 928:1184]
    gq = wi[:, 1184:1696]
    gk = wi[:, 1696:1824]
    gv = wi[:, 1824:1952]
    zeros = lambda n: jnp.zeros((D_MODEL, n), F32)
    pe_slab = jnp.concatenate([zeros(64), k_pe, zeros(32)], axis=1)
    pe_rot_slab = jnp.concatenate([zeros(64), _rot_cols(k_pe, 16), zeros(32)], axis=1)
    wk = jnp.concatenate([c_kv, pe_slab, pe_rot_slab, dk, _rot_cols(dk, 16), gk, _rot_cols(gk, 32)],
                         axis=1)
    wq = jnp.concatenate([q_a, c_kv, dq, _rot_cols(dq, 16), gq, _rot_cols(gq, 32), dv, gv], axis=1)
    qb = w_mla_qb[l].reshape(MLA_Q_RANK, MLA_HEADS, MLA_NOPE + MLA_ROPE)
    qb_pe = qb[:, :, MLA_NOPE:]
    wqb = jnp.concatenate([qb, _rot_cols(qb_pe, 16)], axis=-1).reshape(MLA_Q_RANK, 512)
    kvb = w_mla_kvb[l].reshape(MLA_KV_RANK, MLA_HEADS, MLA_NOPE + MLA_V)
    wkvk = jnp.concatenate([kvb[:, :, :MLA_NOPE], jnp.zeros((MLA_KV_RANK, MLA_HEADS, 64), F32)],
                           axis=-1).reshape(MLA_KV_RANK, 512)
    wkvv = kvb[:, :, MLA_NOPE:].reshape(MLA_KV_RANK, MLA_HEADS * MLA_V)
    bcast = lambda g, n: jnp.broadcast_to(g[:, None], (g.shape[0], n))
    return {
        "g_attn_pre": g_attn_pre[l][None], "g_attn_post": g_attn_post[l][None],
        "g_ffn_pre": g_ffn_pre[l][None], "g_ffn_post": g_ffn_post[l][None],
        "wk": wk.astype(BF16), "wqT": wq.T.astype(BF16), "wqbT": wqb.T.astype(BF16),
        "wkvk": wkvk.astype(BF16), "wkvvT": wkvv.T.astype(BF16),
        "gqmla_b": bcast(g_mla_q[l], TM), "gkv_tok": g_mla_kv[l][None],
        "gkvT_b": bcast(g_mla_kv[l], TM),
        "ggq_b": bcast(g_gqa_q[l], TM), "ggqrot_b": bcast(_swap_halves(g_gqa_q[l], 32), TM),
        "ggk128": jnp.tile(g_gqa_k[l], 2)[None],
        "ggkrot128": jnp.tile(_swap_halves(g_gqa_k[l], 32), 2)[None],
        "gsub_b": bcast(g_diff_sub[l], TQ),
        "lam_vecs": jnp.stack([lam[l] for lam in lambdas]),
        "w_out": w_out[l].astype(BF16), "w_gate": w_ffn_gate[l].astype(BF16),
        "w_up": w_ffn_up[l].astype(BF16), "w_down": w_ffn_down[l].astype(BF16),
    }


def kernel(x, c, ctx, c_ctx, w_ada, b_ada, g_attn_pre, g_attn_post, w_in, g_mla_q, w_mla_qb, g_mla_kv, w_mla_kvb, lambda_q1, lambda_k1, lambda_q2, lambda_k2, g_diff_sub, g_gqa_q, g_gqa_k, w_out, g_ffn_pre, g_ffn_post, w_ffn_gate, w_ffn_up, w_ffn_down):
    bsz = x.shape[0]
    depth = w_ada.shape[0]
    assert x.shape == (bsz, SEQ, D_MODEL) and ctx.shape == (bsz, CTX_LEN, D_MODEL) and bsz == 2
    tabs = _rope_tables()
    cvec = jnp.concatenate([c, c_ctx[None], jnp.zeros((8 - bsz - 1, D_MODEL), F32)], axis=0)
    mod_all = _adaln(cvec, w_ada, b_ada)
    x_all = jnp.concatenate([x, ctx], axis=1)
    for l in range(depth):
        last = l == depth - 1
        lam_init = 0.8 - 0.6 * math.exp(-0.3 * l)
        lw = _layer_weights(l, w_in, g_attn_pre, g_attn_post, g_mla_q, w_mla_qb, g_mla_kv,
                            w_mla_kvb, g_diff_sub, g_gqa_q, g_gqa_k, w_out, g_ffn_pre, g_ffn_post,
                            w_ffn_gate, w_ffn_up, w_ffn_down,
                            (lambda_q1, lambda_k1, lambda_q2, lambda_k2))
        mod3 = mod_all[l].reshape(8, 1, 6 * D_MODEL)
        qT, kslab, vT = _proj(x_all, mod3, lw, tabs)
        yT = _attn(qT, kslab, vT, lw["lam_vecs"], lw["gsub_b"], lam_init, tq=TQ, tk=TK,
                   q_blocks=SEQ // TQ, q_off=0, k_blocks=T_ALL // TK, k_off=0, name="attn")
        yT_ctx = None
        if not last:
            yT_ctx = _attn(qT, kslab, vT, lw["lam_vecs"], lw["gsub_b"][:, :CTX_LEN], lam_init,
                           tq=CTX_LEN, tk=CTX_LEN, q_blocks=1, q_off=SEQ // CTX_LEN, k_blocks=1,
                           k_off=SEQ // CTX_LEN, name="attn_ctx")
        x_all = _out_ffn(yT, yT_ctx, x_all, mod3, lw)
    return x_all
```

```python
import functools
import math

import jax
import jax.numpy as jnp
from jax import lax
from jax.experimental import pallas as pl
from jax.experimental.pallas import tpu as pltpu

F32 = jnp.float32
BF16 = jnp.bfloat16

D_MODEL = 1024
SEQ = 8192
GRID_W = 64
CTX_LEN = 256
T_ALL = SEQ + CTX_LEN
ROPE_THETA = 10000.0
EPS = 1e-6

MLA_HEADS = 4
MLA_Q_RANK = 256
MLA_KV_RANK = 128
MLA_NOPE = 64
MLA_ROPE = 32
MLA_V = 64
DIFF_HEADS = 4
DIFF_QK = 32
DIFF_V = 64
GQA_HEADS = 8
GQA_KV_HEADS = 2
GQA_DIM = 64
HEAD_V = 64
MIX_WIDTH = 1024
FFN_HIDDEN = 2816

LOG2E = 1.4426950408889634
MLA_QS = LOG2E / math.sqrt(MLA_NOPE + MLA_ROPE)
DIFF_QS = LOG2E / math.sqrt(DIFF_QK)
GQA_QS = LOG2E / math.sqrt(GQA_DIM)

LANES = 128
BF16_SUBLANES = 16
VMEM_LIMIT_BYTES = 56 * 1024 * 1024

N_SCORE_HEADS = 20
N_KEY_SLABS = 7
N_VALUE_HEADS = 10
V_ROWS = HEAD_V + BF16_SUBLANES
KEY_SLAB_OF_HEAD = (0, 1, 2, 3, 4, 4, 4, 4, 5, 5, 5, 5, 6, 6, 6, 6, 6, 6, 6, 6)
VALUE_HEAD_OF_HEAD = (0, 1, 2, 3, 4, 4, 5, 5, 6, 6, 7, 7, 8, 8, 8, 8, 9, 9, 9, 9)

TM = 256
TQ = 1024
TK = 768
KEY_CHUNK = 256
STALE_MAX_SLACK = 64.0

WK_COLS = 1152
WQ_ROWS = 2304


def _rms_rows(x, eps=EPS):
    return lax.rsqrt(jnp.mean(x * x, axis=-1, keepdims=True) + eps)


def _rms_cols(x, eps=EPS):
    return lax.rsqrt(jnp.mean(x * x, axis=0, keepdims=True) + eps)


def _adaln_kernel(c_ref, w_ref, b_ref, o_ref):
    c = c_ref[...]
    sc = c * jax.nn.sigmoid(c)
    o_ref[0] = jnp.dot(sc.astype(BF16), w_ref[0].astype(BF16),
                       preferred_element_type=F32) + b_ref[0]


def _adaln(cvec, w_ada, b_ada):
    depth = w_ada.shape[0]
    tn = 1024
    n_out = w_ada.shape[2]
    return pl.pallas_call(
        _adaln_kernel,
        grid=(depth, n_out // tn),
        in_specs=[
            pl.BlockSpec((8, D_MODEL), lambda l, j: (0, 0)),
            pl.BlockSpec((1, D_MODEL, tn), lambda l, j: (l, 0, j)),
            pl.BlockSpec((1, 1, tn), lambda l, j: (l, 0, j)),
        ],
        out_specs=pl.BlockSpec((1, 8, tn), lambda l, j: (l, 0, j)),
        out_shape=jax.ShapeDtypeStruct((depth, 8, n_out), F32),
        compiler_params=pltpu.CompilerParams(
            dimension_semantics=("arbitrary", "arbitrary"),
            vmem_limit_bytes=VMEM_LIMIT_BYTES),
        name="adaln",
    )(cvec, w_ada, b_ada.reshape(depth, 1, n_out))


def _proj_kernel(x_ref, mod_ref, gpre_ref, wk_ref, wqT_ref, wqbT_ref, wkvk_ref, wkvvT_ref,
                 gqmla_ref, gkv_tok_ref, gkvT_ref, ggq_ref, ggqrot_ref, ggk_ref, ggkrot_ref,
                 cS_ref, sS_ref, cL_ref, sL_ref, cST_ref, sST_ref, cLT_ref, sLT_ref,
                 qT_ref, k_ref, vT_ref):
    tm = x_ref.shape[1]
    x = x_ref[0]
    mod = mod_ref[0]
    sh = mod[:, 0:D_MODEL]
    sc = mod[:, D_MODEL:2 * D_MODEL]
    h = (x * _rms_rows(x) * gpre_ref[...]) * (1.0 + sc) + sh
    hb = h.astype(BF16)
    pk = jnp.dot(hb, wk_ref[...], preferred_element_type=F32)
    pT = lax.dot_general(wqT_ref[...], hb, (((1,), (1,)), ((), ())),
                         preferred_element_type=F32)

    cS = cS_ref[...]
    sS = sS_ref[...]
    cL = cL_ref[...]
    sL = sL_ref[...]
    cST = cST_ref[...]
    sST = sST_ref[...]
    cLT = cLT_ref[...]
    sLT = sLT_ref[...]
    ones_rows = jnp.ones((BF16_SUBLANES, tm), BF16)
    zeros32 = jnp.zeros((32, tm), BF16)
    zeros64 = jnp.zeros((64, tm), BF16)

    qaT = pT[0:256]
    qn = (qaT * _rms_cols(qaT) * gqmla_ref[...]).astype(BF16)
    qm = jnp.dot(wqbT_ref[...], qn, preferred_element_type=F32)
    for hh in range(MLA_HEADS):
        blk = qm[128 * hh:128 * (hh + 1)]
        pe = blk[64:96] * cST + blk[96:128] * sST
        qT_ref[0, hh, 0:64, :] = (blk[0:64] * MLA_QS).astype(BF16)
        qT_ref[0, hh, 64:96, :] = (pe * MLA_QS).astype(BF16)
        qT_ref[0, hh, 96:128, :] = zeros32

    ckv = pk[:, 0:128]
    cn = (ckv * _rms_rows(ckv) * gkv_tok_ref[...]).astype(BF16)
    kn = jnp.dot(cn, wkvk_ref[...], preferred_element_type=F32)
    pe_tok = pk[:, 128:256] * cS[:, 0:128] + pk[:, 256:384] * sS[:, 0:128]
    for hh in range(MLA_HEADS):
        k_ref[0, :, 128 * hh:128 * (hh + 1)] = (kn[:, 128 * hh:128 * (hh + 1)] + pe_tok).astype(BF16)

    ckvT = pT[256:384]
    cnT = (ckvT * _rms_cols(ckvT) * gkvT_ref[...]).astype(BF16)
    vmT = jnp.dot(wkvvT_ref[...], cnT, preferred_element_type=F32)
    for hh in range(MLA_HEADS):
        vT_ref[0, hh, 0:64, :] = vmT[64 * hh:64 * (hh + 1)].astype(BF16)
        vT_ref[0, hh, 64:V_ROWS, :] = ones_rows

    qd = pT[384:640].reshape(8, 32, tm)
    qdr = pT[640:896].reshape(8, 32, tm)
    qd = (qd * cST[None] + qdr * sST[None]) * DIFF_QS
    for j in range(8):
        for rb in range(4):
            val = qd[j].astype(BF16) if rb == j % 4 else zeros32
            qT_ref[0, 4 + j, 32 * rb:32 * (rb + 1), :] = val
    kd = pk[:, 384:640] * cS + pk[:, 640:896] * sS
    k_ref[0, :, 512:768] = kd.astype(BF16)
    vdT = pT[1920:2176]
    for hh in range(DIFF_HEADS):
        vT_ref[0, 4 + hh, 0:64, :] = vdT[64 * hh:64 * (hh + 1)].astype(BF16)
        vT_ref[0, 4 + hh, 64:V_ROWS, :] = ones_rows

    gq = pT[896:1408].reshape(8, 64, tm)
    gqr = pT[1408:1920].reshape(8, 64, tm)
    rq = lax.rsqrt(jnp.mean(gq * gq, axis=1, keepdims=True) + EPS)
    qg = ((gq * rq * ggq_ref[...][None]) * cLT[None]
          + (gqr * rq * ggqrot_ref[...][None]) * sLT[None]) * GQA_QS
    for j in range(8):
        grp = j // 4
        for rb in range(2):
            val = qg[j].astype(BF16) if rb == grp else zeros64
            qT_ref[0, 12 + j, 64 * rb:64 * (rb + 1), :] = val
    gk = pk[:, 896:1024]
    gkr = pk[:, 1024:1152]
    sq = gk * gk
    lane = lax.broadcasted_iota(jnp.int32, sq.shape, 1)
    lo = lane < GQA_DIM
    s0 = jnp.sum(jnp.where(lo, sq, 0.0), axis=-1, keepdims=True)
    s1 = jnp.sum(jnp.where(lo, 0.0, sq), axis=-1, keepdims=True)
    rk = jnp.where(lo, lax.rsqrt(s0 / GQA_DIM + EPS), lax.rsqrt(s1 / GQA_DIM + EPS))
    kg = (gk * rk * ggk_ref[...]) * cL + (gkr * rk * ggkrot_ref[...]) * sL
    k_ref[0, :, 768:896] = kg.astype(BF16)
    gvT = pT[2176:2304]
    for hh in range(GQA_KV_HEADS):
        vT_ref[0, 8 + hh, 0:64, :] = gvT[64 * hh:64 * (hh + 1)].astype(BF16)
        vT_ref[0, 8 + hh, 64:V_ROWS, :] = ones_rows


def _proj(x_all, mod3, lw, tabs):
    bsz, t_all, _ = x_all.shape
    nt = t_all // TM
    const = lambda shape: pl.BlockSpec(shape, lambda b, t: (0,) * len(shape))
    in_specs = [
        pl.BlockSpec((1, TM, D_MODEL), lambda b, t: (b, t, 0)),
        pl.BlockSpec((1, 1, 6 * D_MODEL), lambda b, t: (jnp.where(t == nt - 1, 2, b), 0, 0)),
        const((1, D_MODEL)),
        const((D_MODEL, WK_COLS)),
        const((WQ_ROWS, D_MODEL)),
        const((512, 256)),
        const((128, 512)),
        const((256, 128)),
        const((256, TM)),
        const((1, 128)),
        const((128, TM)),
        const((64, TM)),
        const((64, TM)),
        const((1, 128)),
        const((1, 128)),
        pl.BlockSpec((TM, 256), lambda b, t: (t, 0)),
        pl.BlockSpec((TM, 256), lambda b, t: (t, 0)),
        pl.BlockSpec((TM, 128), lambda b, t: (t, 0)),
        pl.BlockSpec((TM, 128), lambda b, t: (t, 0)),
        pl.BlockSpec((32, TM), lambda b, t: (0, t)),
        pl.BlockSpec((32, TM), lambda b, t: (0, t)),
        pl.BlockSpec((64, TM), lambda b, t: (0, t)),
        pl.BlockSpec((64, TM), lambda b, t: (0, t)),
    ]
    out_specs = [
        pl.BlockSpec((1, N_SCORE_HEADS, 128, TM), lambda b, t: (b, 0, 0, t)),
        pl.BlockSpec((1, TM, N_KEY_SLABS * 128), lambda b, t: (b, t, 0)),
        pl.BlockSpec((1, N_VALUE_HEADS, V_ROWS, TM), lambda b, t: (b, 0, 0, t)),
    ]
    out_shape = [
        jax.ShapeDtypeStruct((bsz, N_SCORE_HEADS, 128, t_all), BF16),
        jax.ShapeDtypeStruct((bsz, t_all, N_KEY_SLABS * 128), BF16),
        jax.ShapeDtypeStruct((bsz, N_VALUE_HEADS, V_ROWS, t_all), BF16),
    ]
    return pl.pallas_call(
        _proj_kernel,
        grid=(bsz, nt),
        in_specs=in_specs,
        out_specs=out_specs,
        out_shape=out_shape,
        compiler_params=pltpu.CompilerParams(
            dimension_semantics=("parallel", "parallel"),
            vmem_limit_bytes=VMEM_LIMIT_BYTES),
        name="proj",
    )(x_all, mod3, lw["g_attn_pre"], lw["wk"], lw["wqT"], lw["wqbT"], lw["wkvk"], lw["wkvvT"],
      lw["gqmla_b"], lw["gkv_tok"], lw["gkvT_b"], lw["ggq_b"], lw["ggqrot_b"], lw["ggk128"],
      lw["ggkrot128"],
      tabs["cS"], tabs["sS"], tabs["cL"], tabs["sL"], tabs["cST"], tabs["sST"], tabs["cLT"],
      tabs["sLT"])


def _attn_kernel(qT_ref, k_ref, vT_ref, lam_ref, gsub_ref, yT_ref, acc_ref, m_ref, kept_ref, *,
                 n_k, lam_init):
    ki = pl.program_id(2)
    tk = k_ref.shape[1]
    n_chunks = tk // KEY_CHUNK

    def scores(h, c):
        slab = KEY_SLAB_OF_HEAD[h]
        ks = k_ref[0, KEY_CHUNK * c:KEY_CHUNK * (c + 1), 128 * slab:128 * (slab + 1)]
        return jnp.dot(ks, qT_ref[0, h], preferred_element_type=F32)

    def weighted_values(h, c, pT):
        v = vT_ref[0, VALUE_HEAD_OF_HEAD[h], :, KEY_CHUNK * c:KEY_CHUNK * (c + 1)]
        return jnp.dot(v, pT, preferred_element_type=F32)

    def column_max(m, s_chunks):
        for s in s_chunks:
            m = jnp.maximum(m, jnp.max(s, axis=0, keepdims=True))
        return m

    def sweep_heads(lookahead, head_step):
        pending = [[scores(h, c) for c in range(n_chunks)] for h in range(lookahead)]
        for h in range(N_SCORE_HEADS):
            s_chunks = pending.pop(0)
            issued = []

            def issue_ahead(c, h=h, issued=issued):
                if h + lookahead < N_SCORE_HEADS:
                    issued.append(scores(h + lookahead, c))

            head_step(h, s_chunks, issue_ahead)
            if issued:
                pending.append(issued)

    @pl.when(ki == 0)
    def _first_tile():
        def head_step(h, s_chunks, issue_ahead):
            m_new = column_max(jnp.full((1, s_chunks[0].shape[1]), -jnp.inf, F32), s_chunks)
            oT = None
            for c, s in enumerate(s_chunks):
                issue_ahead(c)
                o_c = weighted_values(h, c, jnp.exp2(s - m_new).astype(BF16))
                oT = o_c if oT is None else oT + o_c
            acc_ref[h] = oT
            m_ref[h] = m_new

        sweep_heads(2, head_step)

    if n_k > 1:
        @pl.when(ki > 0)
        def _later_tiles():
            excess = []

            def head_step(h, s_chunks, issue_ahead):
                m_used = m_ref[h]
                m_tile = m_used
                oT = None
                for c, s in enumerate(s_chunks):
                    issue_ahead(c)
                    m_tile = jnp.maximum(m_tile, jnp.max(s, axis=0, keepdims=True))
                    o_c = weighted_values(h, c, jnp.exp2(s - m_used).astype(BF16))
                    oT = o_c if oT is None else oT + o_c
                acc_old = acc_ref[h]
                over = m_tile - m_used
                keep_old = over > STALE_MAX_SLACK
                acc_ref[h] = jnp.where(keep_old, acc_old, (acc_old + oT) * jnp.exp2(-over))
                m_ref[h] = jnp.where(keep_old, m_used, m_tile)
                kept_ref[h] = jnp.where(keep_old, 1.0, 0.0)
                excess.append(over)

            sweep_heads(1, head_step)
            worst = excess[0]
            for e in excess[1:]:
                worst = jnp.maximum(worst, e)

            @pl.when(jnp.max(worst) > STALE_MAX_SLACK)
            def _redo_kept_columns():
                def redo_head(h, carry):
                    slab = jnp.where(h < 4, h, jnp.where(h < 12, 4 + ((h - 4) >> 2), 6))
                    vh = jnp.where(h < 4, h, jnp.where(h < 12, 4 + ((h - 4) >> 1),
                                                       8 + ((h - 12) >> 2)))
                    ks = k_ref[0, :, pl.ds(pl.multiple_of(slab * 128, 128), 128)]
                    sT = jnp.dot(ks, qT_ref[0, h], preferred_element_type=F32)
                    m_old = m_ref[h]
                    m_new = jnp.maximum(m_old, jnp.max(sT, axis=0, keepdims=True))
                    kept = kept_ref[h] > 0.5
                    oT = jnp.dot(vT_ref[0, vh], jnp.exp2(sT - m_new).astype(BF16),
                                 preferred_element_type=F32)
                    acc_old = acc_ref[h]
                    acc_ref[h] = jnp.where(kept, acc_old * jnp.exp2(m_old - m_new) + oT, acc_old)
                    m_ref[h] = jnp.where(kept, m_new, m_old)
                    return carry

                lax.fori_loop(0, N_SCORE_HEADS, redo_head, 0)

    @pl.when(ki == n_k - 1)
    def _finalize():
        def head_out(h):
            a = acc_ref[h]
            return a[0:HEAD_V] / a[HEAD_V:HEAD_V + 1]
        for hh in range(MLA_HEADS):
            yT_ref[0, 64 * hh:64 * (hh + 1), :] = head_out(hh).astype(BF16)
        lp = lam_ref[...]
        l1 = jnp.sum(lp[0:1] * lp[1:2], axis=-1, keepdims=True)
        l2 = jnp.sum(lp[2:3] * lp[3:4], axis=-1, keepdims=True)
        lam = jnp.exp(l1) - jnp.exp(l2) + lam_init
        gsub = gsub_ref[...]
        for hh in range(DIFF_HEADS):
            d = head_out(4 + 2 * hh) - lam * head_out(5 + 2 * hh)
            y = (d * _rms_cols(d) * gsub) * (1.0 - lam_init)
            yT_ref[0, 256 + 64 * hh:256 + 64 * (hh + 1), :] = y.astype(BF16)
        for j in range(GQA_HEADS):
            yT_ref[0, 512 + 64 * j:512 + 64 * (j + 1), :] = head_out(12 + j).astype(BF16)


def _attn(qT, kslab, vT, lam_vecs, gsub_b, lam_init, *, tq, tk, q_blocks, q_off, k_blocks, k_off,
          name):
    bsz = qT.shape[0]
    return pl.pallas_call(
        functools.partial(_attn_kernel, n_k=k_blocks, lam_init=lam_init),
        grid=(bsz, q_blocks, k_blocks),
        in_specs=[
            pl.BlockSpec((1, N_SCORE_HEADS, 128, tq), lambda b, qi, ki: (b, 0, 0, qi + q_off)),
            pl.BlockSpec((1, tk, N_KEY_SLABS * 128), lambda b, qi, ki: (b, ki + k_off, 0)),
            pl.BlockSpec((1, N_VALUE_HEADS, V_ROWS, tk), lambda b, qi, ki: (b, 0, 0, ki + k_off)),
            pl.BlockSpec((4, DIFF_QK), lambda b, qi, ki: (0, 0)),
            pl.BlockSpec((HEAD_V, tq), lambda b, qi, ki: (0, 0)),
        ],
        out_specs=pl.BlockSpec((1, MIX_WIDTH, tq), lambda b, qi, ki: (b, 0, qi)),
        out_shape=jax.ShapeDtypeStruct((bsz, MIX_WIDTH, q_blocks * tq), BF16),
        scratch_shapes=[
            pltpu.VMEM((N_SCORE_HEADS, V_ROWS, tq), F32),
            pltpu.VMEM((N_SCORE_HEADS, 1, tq), F32),
            pltpu.VMEM((N_SCORE_HEADS, 1, tq), F32),
        ],
        compiler_params=pltpu.CompilerParams(
            dimension_semantics=("parallel", "parallel", "arbitrary"),
            vmem_limit_bytes=VMEM_LIMIT_BYTES),
        name=name,
    )(qT, kslab, vT, lam_vecs, gsub_b)


def _out_ffn_kernel(*refs, n_lat_tiles, has_ctx):
    if has_ctx:
        yT_ref, yTc_ref = refs[:2]
        refs = refs[2:]
        yT = jnp.where(pl.program_id(1) == n_lat_tiles, yTc_ref[0], yT_ref[0])
    else:
        yT = refs[0][0]
        refs = refs[1:]
    x_ref, mod_ref, gpost_ref, gfpre_ref, gfpost_ref, wout_ref, wg_ref, wu_ref, wd_ref, o_ref = refs
    yp = lax.dot_general(yT, wout_ref[...], (((0,), (0,)), ((), ())),
                         preferred_element_type=F32)
    mod = mod_ref[0]
    gt_a = mod[:, 2 * D_MODEL:3 * D_MODEL]
    sh_f = mod[:, 3 * D_MODEL:4 * D_MODEL]
    sc_f = mod[:, 4 * D_MODEL:5 * D_MODEL]
    gt_f = mod[:, 5 * D_MODEL:6 * D_MODEL]
    x1 = x_ref[0] + gt_a * (yp * _rms_rows(yp) * gpost_ref[...])
    hf = (x1 * _rms_rows(x1) * gfpre_ref[...]) * (1.0 + sc_f) + sh_f
    hb = hf.astype(BF16)
    g = jnp.dot(hb, wg_ref[...], preferred_element_type=F32)
    u = jnp.dot(hb, wu_ref[...], preferred_element_type=F32)
    a = (g * jax.nn.sigmoid(g)) * u
    f = jnp.dot(a.astype(BF16), wd_ref[...], preferred_element_type=F32)
    o_ref[0] = x1 + gt_f * (f * _rms_rows(f) * gfpost_ref[...])


def _out_ffn(yT, yT_ctx, x_all, mod3, lw):
    bsz = x_all.shape[0]
    nt_all = x_all.shape[1] // TM
    n_lat = SEQ // TM
    has_ctx = yT_ctx is not None
    n_tiles = n_lat + 1 if has_ctx else n_lat
    const = lambda shape: pl.BlockSpec(shape, lambda b, t: (0,) * len(shape),
                                       pipeline_mode=pl.Buffered(1))
    y_specs = [pl.BlockSpec((1, MIX_WIDTH, TM), lambda b, t: (b, 0, jnp.minimum(t, n_lat - 1)))]
    y_args = [yT]
    if has_ctx:
        y_specs.append(pl.BlockSpec((1, MIX_WIDTH, TM), lambda b, t: (b, 0, 0)))
        y_args.append(yT_ctx)
    return pl.pallas_call(
        functools.partial(_out_ffn_kernel, n_lat_tiles=n_lat, has_ctx=has_ctx),
        grid=(bsz, n_tiles),
        in_specs=y_specs + [
            pl.BlockSpec((1, TM, D_MODEL), lambda b, t: (b, t, 0)),
            pl.BlockSpec((1, 1, 6 * D_MODEL), lambda b, t: (jnp.where(t == nt_all - 1, 2, b), 0, 0)),
            const((1, D_MODEL)),
            const((1, D_MODEL)),
            const((1, D_MODEL)),
            const((MIX_WIDTH, D_MODEL)),
            const((D_MODEL, FFN_HIDDEN)),
            const((D_MODEL, FFN_HIDDEN)),
            const((FFN_HIDDEN, D_MODEL)),
        ],
        out_specs=pl.BlockSpec((1, TM, D_MODEL), lambda b, t: (b, t, 0)),
        out_shape=jax.ShapeDtypeStruct((bsz, n_tiles * TM, D_MODEL), F32),
        compiler_params=pltpu.CompilerParams(
            dimension_semantics=("parallel", "parallel"),
            vmem_limit_bytes=VMEM_LIMIT_BYTES),
        name="out_ffn",
    )(*y_args, x_all, mod3, lw["g_attn_post"], lw["g_ffn_pre"], lw["g_ffn_post"],
      lw["w_out"], lw["w_gate"], lw["w_up"], lw["w_down"])


def _rot_cols(w, half):
    n = w.shape[-1]
    wg = w.reshape(w.shape[:-1] + (n // (2 * half), 2, half))
    return jnp.concatenate([-wg[..., 1:2, :], wg[..., 0:1, :]], axis=-2).reshape(w.shape)


def _swap_halves(g, half):
    return jnp.concatenate([g[half:], g[:half]])


def _rope_tables():
    t = jnp.arange(SEQ, dtype=jnp.int32)
    row = (t // GRID_W).astype(F32)
    col = (t % GRID_W).astype(F32)

    def tables(rot_dim):
        quarter = rot_dim // 4
        inv = ROPE_THETA ** (-jnp.arange(quarter, dtype=F32) / quarter)
        ang = jnp.concatenate([row[:, None] * inv, col[:, None] * inv], axis=-1)
        cos = jnp.concatenate([jnp.cos(ang), jnp.ones((CTX_LEN, rot_dim // 2), F32)], axis=0)
        sin = jnp.concatenate([jnp.sin(ang), jnp.zeros((CTX_LEN, rot_dim // 2), F32)], axis=0)
        return jnp.tile(cos, (1, 2)), jnp.tile(sin, (1, 2))

    c_s, s_s = tables(MLA_ROPE)
    c_l, s_l = tables(GQA_DIM)
    return {
        "cS": jnp.tile(c_s, (1, 8)), "sS": jnp.tile(s_s, (1, 8)),
        "cL": jnp.tile(c_l, (1, 2)), "sL": jnp.tile(s_l, (1, 2)),
        "cST": c_s.T, "sST": s_s.T,
        "cLT": c_l.T, "sLT": s_l.T,
    }


def _layer_weights(l, w_in, g_attn_pre, g_attn_post, g_mla_q, w_mla_qb, g_mla_kv, w_mla_kvb,
                   g_diff_sub, g_gqa_q, g_gqa_k, w_out, g_ffn_pre, g_ffn_post, w_ffn_gate,
                   w_ffn_up, w_ffn_down, lambdas):
    wi = w_in[l]
    q_a = wi[:, 0:256]
    c_kv = wi[:, 256:384]
    k_pe = wi[:, 384:416]
    dq = wi[:, 416:672]
    dk = wi[:, 672:928]
    dv = wi[:, 928:1184]
    gq = wi[:, 1184:1696]
    gk = wi[:, 1696:1824]
    gv = wi[:, 1824:1952]
    zeros = lambda n: jnp.zeros((D_MODEL, n), F32)
    pe_slab = jnp.concatenate([zeros(64), k_pe, zeros(32)], axis=1)
    pe_rot_slab = jnp.concatenate([zeros(64), _rot_cols(k_pe, 16), zeros(32)], axis=1)
    wk = jnp.concatenate([c_kv, pe_slab, pe_rot_slab, dk, _rot_cols(dk, 16), gk, _rot_cols(gk, 32)],
                         axis=1)
    wq = jnp.concatenate([q_a, c_kv, dq, _rot_cols(dq, 16), gq, _rot_cols(gq, 32), dv, gv], axis=1)
    qb = w_mla_qb[l].reshape(MLA_Q_RANK, MLA_HEADS, MLA_NOPE + MLA_ROPE)
    qb_pe = qb[:, :, MLA_NOPE:]
    wqb = jnp.concatenate([qb, _rot_cols(qb_pe, 16)], axis=-1).reshape(MLA_Q_RANK, 512)
    kvb = w_mla_kvb[l].reshape(MLA_KV_RANK, MLA_HEADS, MLA_NOPE + MLA_V)
    wkvk = jnp.concatenate([kvb[:, :, :MLA_NOPE], jnp.zeros((MLA_KV_RANK, MLA_HEADS, 64), F32)],
                           axis=-1).reshape(MLA_KV_RANK, 512)
    wkvv = kvb[:, :, MLA_NOPE:].reshape(MLA_KV_RANK, MLA_HEADS * MLA_V)
    bcast = lambda g, n: jnp.broadcast_to(g[:, None], (g.shape[0], n))
    return {
        "g_attn_pre": g_attn_pre[l][None], "g_attn_post": g_attn_post[l][None],
        "g_ffn_pre": g_ffn_pre[l][None], "g_ffn_post": g_ffn_post[l][None],
        "wk": wk.astype(BF16), "wqT": wq.T.astype(BF16), "wqbT": wqb.T.astype(BF16),
        "wkvk": wkvk.astype(BF16), "wkvvT": wkvv.T.astype(BF16),
        "gqmla_b": bcast(g_mla_q[l], TM), "gkv_tok": g_mla_kv[l][None],
        "gkvT_b": bcast(g_mla_kv[l], TM),
        "ggq_b": bcast(g_gqa_q[l], TM), "ggqrot_b": bcast(_swap_halves(g_gqa_q[l], 32), TM),
        "ggk128": jnp.tile(g_gqa_k[l], 2)[None],
        "ggkrot128": jnp.tile(_swap_halves(g_gqa_k[l], 32), 2)[None],
        "gsub_b": bcast(g_diff_sub[l], TQ),
        "lam_vecs": jnp.stack([lam[l] for lam in lambdas]),
        "w_out": w_out[l].astype(BF16), "w_gate": w_ffn_gate[l].astype(BF16),
        "w_up": w_ffn_up[l].astype(BF16), "w_down": w_ffn_down[l].astype(BF16),
    }


def kernel(x, c, ctx, c_ctx, w_ada, b_ada, g_attn_pre, g_attn_post, w_in, g_mla_q, w_mla_qb, g_mla_kv, w_mla_kvb, lambda_q1, lambda_k1, lambda_q2, lambda_k2, g_diff_sub, g_gqa_q, g_gqa_k, w_out, g_ffn_pre, g_ffn_post, w_ffn_gate, w_ffn_up, w_ffn_down):
    bsz = x.shape[0]
    depth = w_ada.shape[0]
    assert x.shape == (bsz, SEQ, D_MODEL) and ctx.shape == (bsz, CTX_LEN, D_MODEL) and bsz == 2
    tabs = _rope_tables()
    cvec = jnp.concatenate([c, c_ctx[None], jnp.zeros((8 - bsz - 1, D_MODEL), F32)], axis=0)
    mod_all = _adaln(cvec, w_ada, b_ada)
    x_all = jnp.concatenate([x, ctx], axis=1)
    for l in range(depth):
        last = l == depth - 1
        lam_init = 0.8 - 0.6 * math.exp(-0.3 * l)
        lw = _layer_weights(l, w_in, g_attn_pre, g_attn_post, g_mla_q, w_mla_qb, g_mla_kv,
                            w_mla_kvb, g_diff_sub, g_gqa_q, g_gqa_k, w_out, g_ffn_pre, g_ffn_post,
                            w_ffn_gate, w_ffn_up, w_ffn_down,
                            (lambda_q1, lambda_k1, lambda_q2, lambda_k2))
        mod3 = mod_all[l].reshape(8, 1, 6 * D_MODEL)
        qT, kslab, vT = _proj(x_all, mod3, lw, tabs)
        yT = _attn(qT, kslab, vT, lw["lam_vecs"], lw["gsub_b"], lam_init, tq=TQ, tk=TK,
                   q_blocks=SEQ // TQ, q_off=0, k_blocks=T_ALL // TK, k_off=0, name="attn")
        yT_ctx = None
        if not last:
            yT_ctx = _attn(qT, kslab, vT, lw["lam_vecs"], lw["gsub_b"][:, :CTX_LEN], lam_init,
                           tq=CTX_LEN, tk=CTX_LEN, q_blocks=1, q_off=SEQ // CTX_LEN, k_blocks=1,
                           k_off=SEQ // CTX_LEN, name="attn_ctx")
        x_all = _out_ffn(yT, yT_ctx, x_all, mod3, lw)
    return x_all
```

```python
import functools
import math

import jax
import jax.numpy as jnp
from jax import lax
from jax.experimental import pallas as pl
from jax.experimental.pallas import tpu as pltpu

F32 = jnp.float32
BF16 = jnp.bfloat16

D_MODEL = 1024
SEQ = 8192
GRID_W = 64
CTX_LEN = 256
T_ALL = SEQ + CTX_LEN
ROPE_THETA = 10000.0
EPS = 1e-6

MLA_HEADS = 4
MLA_Q_RANK = 256
MLA_KV_RANK = 128
MLA_NOPE = 64
MLA_ROPE = 32
MLA_V = 64
DIFF_HEADS = 4
DIFF_QK = 32
DIFF_V = 64
GQA_HEADS = 8
GQA_KV_HEADS = 2
GQA_DIM = 64
HEAD_V = 64
MIX_WIDTH = 1024
FFN_HIDDEN = 2816

LOG2E = 1.4426950408889634
MLA_QS = LOG2E / math.sqrt(MLA_NOPE + MLA_ROPE)
DIFF_QS = LOG2E / math.sqrt(DIFF_QK)
GQA_QS = LOG2E / math.sqrt(GQA_DIM)

LANES = 128
BF16_SUBLANES = 16
VMEM_LIMIT_BYTES = 56 * 1024 * 1024

N_SCORE_HEADS = 20
N_KEY_SLABS = 7
N_VALUE_HEADS = 10
V_ROWS = HEAD_V + BF16_SUBLANES
KEY_SLAB_OF_HEAD = (0, 1, 2, 3, 4, 4, 4, 4, 5, 5, 5, 5, 6, 6, 6, 6, 6, 6, 6, 6)
VALUE_HEAD_OF_HEAD = (0, 1, 2, 3, 4, 4, 5, 5, 6, 6, 7, 7, 8, 8, 8, 8, 9, 9, 9, 9)

TM = 256
TQ = 512
TK = 768
KEY_CHUNK = 256
STALE_MAX_SLACK = 64.0

WK_COLS = 1152
WQ_ROWS = 2304


def _rms_rows(x, eps=EPS):
    return lax.rsqrt(jnp.mean(x * x, axis=-1, keepdims=True) + eps)


def _rms_cols(x, eps=EPS):
    return lax.rsqrt(jnp.mean(x * x, axis=0, keepdims=True) + eps)


def _adaln_kernel(c_ref, w_ref, b_ref, o_ref):
    c = c_ref[...]
    sc = c * jax.nn.sigmoid(c)
    o_ref[0] = jnp.dot(sc.astype(BF16), w_ref[0].astype(BF16),
                       preferred_element_type=F32) + b_ref[0]


def _adaln(cvec, w_ada, b_ada):
    depth = w_ada.shape[0]
    tn = 1024
    n_out = w_ada.shape[2]
    return pl.pallas_call(
        _adaln_kernel,
        grid=(depth, n_out // tn),
        in_specs=[
            pl.BlockSpec((8, D_MODEL), lambda l, j: (0, 0)),
            pl.BlockSpec((1, D_MODEL, tn), lambda l, j: (l, 0, j)),
            pl.BlockSpec((1, 1, tn), lambda l, j: (l, 0, j)),
        ],
        out_specs=pl.BlockSpec((1, 8, tn), lambda l, j: (l, 0, j)),
        out_shape=jax.ShapeDtypeStruct((depth, 8, n_out), F32),
        compiler_params=pltpu.CompilerParams(
            dimension_semantics=("arbitrary", "arbitrary"),
            vmem_limit_bytes=VMEM_LIMIT_BYTES),
        name="adaln",
    )(cvec, w_ada, b_ada.reshape(depth, 1, n_out))


def _proj_kernel(x_ref, mod_ref, gpre_ref, wk_ref, wqT_ref, wqbT_ref, wkvk_ref, wkvvT_ref,
                 gqmla_ref, gkv_tok_ref, gkvT_ref, ggq_ref, ggqrot_ref, ggk_ref, ggkrot_ref,
                 cS_ref, sS_ref, cL_ref, sL_ref, cST_ref, sST_ref, cLT_ref, sLT_ref,
                 qT_ref, k_ref, vT_ref):
    tm = x_ref.shape[1]
    x = x_ref[0]
    mod = mod_ref[0]
    sh = mod[:, 0:D_MODEL]
    sc = mod[:, D_MODEL:2 * D_MODEL]
    h = (x * _rms_rows(x) * gpre_ref[...]) * (1.0 + sc) + sh
    hb = h.astype(BF16)
    pk = jnp.dot(hb, wk_ref[...], preferred_element_type=F32)
    pT = lax.dot_general(wqT_ref[...], hb, (((1,), (1,)), ((), ())),
                         preferred_element_type=F32)

    cS = cS_ref[...]
    sS = sS_ref[...]
    cL = cL_ref[...]
    sL = sL_ref[...]
    cST = cST_ref[...]
    sST = sST_ref[...]
    cLT = cLT_ref[...]
    sLT = sLT_ref[...]
    ones_rows = jnp.ones((BF16_SUBLANES, tm), BF16)
    zeros32 = jnp.zeros((32, tm), BF16)
    zeros64 = jnp.zeros((64, tm), BF16)

    qaT = pT[0:256]
    qn = (qaT * _rms_cols(qaT) * gqmla_ref[...]).astype(BF16)
    qm = jnp.dot(wqbT_ref[...], qn, preferred_element_type=F32)
    for hh in range(MLA_HEADS):
        blk = qm[128 * hh:128 * (hh + 1)]
        pe = blk[64:96] * cST + blk[96:128] * sST
        qT_ref[0, hh, 0:64, :] = (blk[0:64] * MLA_QS).astype(BF16)
        qT_ref[0, hh, 64:96, :] = (pe * MLA_QS).astype(BF16)
        qT_ref[0, hh, 96:128, :] = zeros32

    ckv = pk[:, 0:128]
    cn = (ckv * _rms_rows(ckv) * gkv_tok_ref[...]).astype(BF16)
    kn = jnp.dot(cn, wkvk_ref[...], preferred_element_type=F32)
    pe_tok = pk[:, 128:256] * cS[:, 0:128] + pk[:, 256:384] * sS[:, 0:128]
    for hh in range(MLA_HEADS):
        k_ref[0, :, 128 * hh:128 * (hh + 1)] = (kn[:, 128 * hh:128 * (hh + 1)] + pe_tok).astype(BF16)

    ckvT = pT[256:384]
    cnT = (ckvT * _rms_cols(ckvT) * gkvT_ref[...]).astype(BF16)
    vmT = jnp.dot(wkvvT_ref[...], cnT, preferred_element_type=F32)
    for hh in range(MLA_HEADS):
        vT_ref[0, hh, 0:64, :] = vmT[64 * hh:64 * (hh + 1)].astype(BF16)
        vT_ref[0, hh, 64:V_ROWS, :] = ones_rows

    qd = pT[384:640].reshape(8, 32, tm)
    qdr = pT[640:896].reshape(8, 32, tm)
    qd = (qd * cST[None] + qdr * sST[None]) * DIFF_QS
    for j in range(8):
        for rb in range(4):
            val = qd[j].astype(BF16) if rb == j % 4 else zeros32
            qT_ref[0, 4 + j, 32 * rb:32 * (rb + 1), :] = val
    kd = pk[:, 384:640] * cS + pk[:, 640:896] * sS
    k_ref[0, :, 512:768] = kd.astype(BF16)
    vdT = pT[1920:2176]
    for hh in range(DIFF_HEADS):
        vT_ref[0, 4 + hh, 0:64, :] = vdT[64 * hh:64 * (hh + 1)].astype(BF16)
        vT_ref[0, 4 + hh, 64:V_ROWS, :] = ones_rows

    gq = pT[896:1408].reshape(8, 64, tm)
    gqr = pT[1408:1920].reshape(8, 64, tm)
    rq = lax.rsqrt(jnp.mean(gq * gq, axis=1, keepdims=True) + EPS)
    qg = ((gq * rq * ggq_ref[...][None]) * cLT[None]
          + (gqr * rq * ggqrot_ref[...][None]) * sLT[None]) * GQA_QS
    for j in range(8):
        grp = j // 4
        for rb in range(2):
            val = qg[j].astype(BF16) if rb == grp else zeros64
            qT_ref[0, 12 + j, 64 * rb:64 * (rb + 1), :] = val
    gk = pk[:, 896:1024]
    gkr = pk[:, 1024:1152]
    sq = gk * gk
    lane = lax.broadcasted_iota(jnp.int32, sq.shape, 1)
    lo = lane < GQA_DIM
    s0 = jnp.sum(jnp.where(lo, sq, 0.0), axis=-1, keepdims=True)
    s1 = jnp.sum(jnp.where(lo, 0.0, sq), axis=-1, keepdims=True)
    rk = jnp.where(lo, lax.rsqrt(s0 / GQA_DIM + EPS), lax.rsqrt(s1 / GQA_DIM + EPS))
    kg = (gk * rk * ggk_ref[...]) * cL + (gkr * rk * ggkrot_ref[...]) * sL
    k_ref[0, :, 768:896] = kg.astype(BF16)
    gvT = pT[2176:2304]
    for hh in range(GQA_KV_HEADS):
        vT_ref[0, 8 + hh, 0:64, :] = gvT[64 * hh:64 * (hh + 1)].astype(BF16)
        vT_ref[0, 8 + hh, 64:V_ROWS, :] = ones_rows


def _proj(x_all, mod3, lw, tabs):
    bsz, t_all, _ = x_all.shape
    nt = t_all // TM
    const = lambda shape: pl.BlockSpec(shape, lambda b, t: (0,) * len(shape))
    in_specs = [
        pl.BlockSpec((1, TM, D_MODEL), lambda b, t: (b, t, 0)),
        pl.BlockSpec((1, 1, 6 * D_MODEL), lambda b, t: (jnp.where(t == nt - 1, 2, b), 0, 0)),
        const((1, D_MODEL)),
        const((D_MODEL, WK_COLS)),
        const((WQ_ROWS, D_MODEL)),
        const((512, 256)),
        const((128, 512)),
        const((256, 128)),
        const((256, TM)),
        const((1, 128)),
        const((128, TM)),
        const((64, TM)),
        const((64, TM)),
        const((1, 128)),
        const((1, 128)),
        pl.BlockSpec((TM, 256), lambda b, t: (t, 0)),
        pl.BlockSpec((TM, 256), lambda b, t: (t, 0)),
        pl.BlockSpec((TM, 128), lambda b, t: (t, 0)),
        pl.BlockSpec((TM, 128), lambda b, t: (t, 0)),
        pl.BlockSpec((32, TM), lambda b, t: (0, t)),
        pl.BlockSpec((32, TM), lambda b, t: (0, t)),
        pl.BlockSpec((64, TM), lambda b, t: (0, t)),
        pl.BlockSpec((64, TM), lambda b, t: (0, t)),
    ]
    out_specs = [
        pl.BlockSpec((1, N_SCORE_HEADS, 128, TM), lambda b, t: (b, 0, 0, t)),
        pl.BlockSpec((1, TM, N_KEY_SLABS * 128), lambda b, t: (b, t, 0)),
        pl.BlockSpec((1, N_VALUE_HEADS, V_ROWS, TM), lambda b, t: (b, 0, 0, t)),
    ]
    out_shape = [
        jax.ShapeDtypeStruct((bsz, N_SCORE_HEADS, 128, t_all), BF16),
        jax.ShapeDtypeStruct((bsz, t_all, N_KEY_SLABS * 128), BF16),
        jax.ShapeDtypeStruct((bsz, N_VALUE_HEADS, V_ROWS, t_all), BF16),
    ]
    return pl.pallas_call(
        _proj_kernel,
        grid=(bsz, nt),
        in_specs=in_specs,
        out_specs=out_specs,
        out_shape=out_shape,
        compiler_params=pltpu.CompilerParams(
            dimension_semantics=("parallel", "parallel"),
            vmem_limit_bytes=VMEM_LIMIT_BYTES),
        name="proj",
    )(x_all, mod3, lw["g_attn_pre"], lw["wk"], lw["wqT"], lw["wqbT"], lw["wkvk"], lw["wkvvT"],
      lw["gqmla_b"], lw["gkv_tok"], lw["gkvT_b"], lw["ggq_b"], lw["ggqrot_b"], lw["ggk128"],
      lw["ggkrot128"],
      tabs["cS"], tabs["sS"], tabs["cL"], tabs["sL"], tabs["cST"], tabs["sST"], tabs["cLT"],
      tabs["sLT"])


def _attn_kernel(qT_ref, k_ref, vT_ref, lam_ref, gsub_ref, yT_ref, acc_ref, m_ref, kept_ref, *,
                 n_k, lam_init):
    ki = pl.program_id(2)
    tk = k_ref.shape[1]
    n_chunks = tk // KEY_CHUNK

    def scores(h, c):
        slab = KEY_SLAB_OF_HEAD[h]
        ks = k_ref[0, KEY_CHUNK * c:KEY_CHUNK * (c + 1), 128 * slab:128 * (slab + 1)]
        return jnp.dot(ks, qT_ref[0, h], preferred_element_type=F32)

    def weighted_values(h, c, pT):
        v = vT_ref[0, VALUE_HEAD_OF_HEAD[h], :, KEY_CHUNK * c:KEY_CHUNK * (c + 1)]
        return jnp.dot(v, pT, preferred_element_type=F32)

    def column_max(m, s_chunks):
        for s in s_chunks:
            m = jnp.maximum(m, jnp.max(s, axis=0, keepdims=True))
        return m

    def sweep_heads(lookahead, head_step):
        pending = [[scores(h, c) for c in range(n_chunks)] for h in range(lookahead)]
        for h in range(N_SCORE_HEADS):
            s_chunks = pending.pop(0)
            issued = []

            def issue_ahead(c, h=h, issued=issued):
                if h + lookahead < N_SCORE_HEADS:
                    issued.append(scores(h + lookahead, c))

            head_step(h, s_chunks, issue_ahead)
            if issued:
                pending.append(issued)

    @pl.when(ki == 0)
    def _first_tile():
        def head_step(h, s_chunks, issue_ahead):
            m_new = column_max(jnp.full((1, s_chunks[0].shape[1]), -jnp.inf, F32), s_chunks)
            oT = None
            for c, s in enumerate(s_chunks):
                issue_ahead(c)
                o_c = weighted_values(h, c, jnp.exp2(s - m_new).astype(BF16))
                oT = o_c if oT is None else oT + o_c
            acc_ref[h] = oT
            m_ref[h] = m_new

        sweep_heads(2, head_step)

    if n_k > 1:
        @pl.when(ki > 0)
        def _later_tiles():
            excess = []

            def head_step(h, s_chunks, issue_ahead):
                m_used = m_ref[h]
                m_tile = m_used
                oT = None
                for c, s in enumerate(s_chunks):
                    m_tile = jnp.maximum(m_tile, jnp.max(s, axis=0, keepdims=True))
                    o_c = weighted_values(h, c, jnp.exp2(s - m_used).astype(BF16))
                    issue_ahead(c)
                    oT = o_c if oT is None else oT + o_c
                acc_old = acc_ref[h]
                over = m_tile - m_used
                keep_old = over > STALE_MAX_SLACK
                acc_ref[h] = jnp.where(keep_old, acc_old, (acc_old + oT) * jnp.exp2(-over))
                m_ref[h] = jnp.where(keep_old, m_used, m_tile)
                kept_ref[h] = jnp.where(keep_old, 1.0, 0.0)
                excess.append(over)

            sweep_heads(1, head_step)
            worst = excess[0]
            for e in excess[1:]:
                worst = jnp.maximum(worst, e)

            @pl.when(jnp.max(worst) > STALE_MAX_SLACK)
            def _redo_kept_columns():
                def redo_head(h, carry):
                    slab = jnp.where(h < 4, h, jnp.where(h < 12, 4 + ((h - 4) >> 2), 6))
                    vh = jnp.where(h < 4, h, jnp.where(h < 12, 4 + ((h - 4) >> 1),
                                                       8 + ((h - 12) >> 2)))
                    ks = k_ref[0, :, pl.ds(pl.multiple_of(slab * 128, 128), 128)]
                    sT = jnp.dot(ks, qT_ref[0, h], preferred_element_type=F32)
                    m_old = m_ref[h]
                    m_new = jnp.maximum(m_old, jnp.max(sT, axis=0, keepdims=True))
                    kept = kept_ref[h] > 0.5
                    oT = jnp.dot(vT_ref[0, vh], jnp.exp2(sT - m_new).astype(BF16),
                                 preferred_element_type=F32)
                    acc_old = acc_ref[h]
                    acc_ref[h] = jnp.where(kept, acc_old * jnp.exp2(m_old - m_new) + oT, acc_old)
                    m_ref[h] = jnp.where(kept, m_new, m_old)
                    return carry

                lax.fori_loop(0, N_SCORE_HEADS, redo_head, 0)

    @pl.when(ki == n_k - 1)
    def _finalize():
        def head_out(h):
            a = acc_ref[h]
            return a[0:HEAD_V] / a[HEAD_V:HEAD_V + 1]
        for hh in range(MLA_HEADS):
            yT_ref[0, 64 * hh:64 * (hh + 1), :] = head_out(hh).astype(BF16)
        lp = lam_ref[...]
        l1 = jnp.sum(lp[0:1] * lp[1:2], axis=-1, keepdims=True)
        l2 = jnp.sum(lp[2:3] * lp[3:4], axis=-1, keepdims=True)
        lam = jnp.exp(l1) - jnp.exp(l2) + lam_init
        gsub = gsub_ref[...]
        for hh in range(DIFF_HEADS):
            d = head_out(4 + 2 * hh) - lam * head_out(5 + 2 * hh)
            y = (d * _rms_cols(d) * gsub) * (1.0 - lam_init)
            yT_ref[0, 256 + 64 * hh:256 + 64 * (hh + 1), :] = y.astype(BF16)
        for j in range(GQA_HEADS):
            yT_ref[0, 512 + 64 * j:512 + 64 * (j + 1), :] = head_out(12 + j).astype(BF16)


def _attn(qT, kslab, vT, lam_vecs, gsub_b, lam_init, *, tq, tk, q_blocks, q_off, k_blocks, k_off,
          name):
    bsz = qT.shape[0]
    return pl.pallas_call(
        functools.partial(_attn_kernel, n_k=k_blocks, lam_init=lam_init),
        grid=(bsz, q_blocks, k_blocks),
        in_specs=[
            pl.BlockSpec((1, N_SCORE_HEADS, 128, tq), lambda b, qi, ki: (b, 0, 0, qi + q_off)),
            pl.BlockSpec((1, tk, N_KEY_SLABS * 128), lambda b, qi, ki: (b, ki + k_off, 0)),
            pl.BlockSpec((1, N_VALUE_HEADS, V_ROWS, tk), lambda b, qi, ki: (b, 0, 0, ki + k_off)),
            pl.BlockSpec((4, DIFF_QK), lambda b, qi, ki: (0, 0)),
            pl.BlockSpec((HEAD_V, tq), lambda b, qi, ki: (0, 0)),
        ],
        out_specs=pl.BlockSpec((1, MIX_WIDTH, tq), lambda b, qi, ki: (b, 0, qi)),
        out_shape=jax.ShapeDtypeStruct((bsz, MIX_WIDTH, q_blocks * tq), BF16),
        scratch_shapes=[
            pltpu.VMEM((N_SCORE_HEADS, V_ROWS, tq), F32),
            pltpu.VMEM((N_SCORE_HEADS, 1, tq), F32),
            pltpu.VMEM((N_SCORE_HEADS, 1, tq), F32),
        ],
        compiler_params=pltpu.CompilerParams(
            dimension_semantics=("parallel", "parallel", "arbitrary"),
            vmem_limit_bytes=VMEM_LIMIT_BYTES),
        name=name,
    )(qT, kslab, vT, lam_vecs, gsub_b)


def _out_ffn_kernel(*refs, n_lat_tiles, has_ctx):
    if has_ctx:
        yT_ref, yTc_ref = refs[:2]
        refs = refs[2:]
        yT = jnp.where(pl.program_id(1) == n_lat_tiles, yTc_ref[0], yT_ref[0])
    else:
        yT = refs[0][0]
        refs = refs[1:]
    x_ref, mod_ref, gpost_ref, gfpre_ref, gfpost_ref, wout_ref, wg_ref, wu_ref, wd_ref, o_ref = refs
    yp = lax.dot_general(yT, wout_ref[...], (((0,), (0,)), ((), ())),
                         preferred_element_type=F32)
    mod = mod_ref[0]
    gt_a = mod[:, 2 * D_MODEL:3 * D_MODEL]
    sh_f = mod[:, 3 * D_MODEL:4 * D_MODEL]
    sc_f = mod[:, 4 * D_MODEL:5 * D_MODEL]
    gt_f = mod[:, 5 * D_MODEL:6 * D_MODEL]
    x1 = x_ref[0] + gt_a * (yp * _rms_rows(yp) * gpost_ref[...])
    hf = (x1 * _rms_rows(x1) * gfpre_ref[...]) * (1.0 + sc_f) + sh_f
    hb = hf.astype(BF16)
    g = jnp.dot(hb, wg_ref[...], preferred_element_type=F32)
    u = jnp.dot(hb, wu_ref[...], preferred_element_type=F32)
    a = (g * jax.nn.sigmoid(g)) * u
    f = jnp.dot(a.astype(BF16), wd_ref[...], preferred_element_type=F32)
    o_ref[0] = x1 + gt_f * (f * _rms_rows(f) * gfpost_ref[...])


def _out_ffn(yT, yT_ctx, x_all, mod3, lw):
    bsz = x_all.shape[0]
    nt_all = x_all.shape[1] // TM
    n_lat = SEQ // TM
    has_ctx = yT_ctx is not None
    n_tiles = n_lat + 1 if has_ctx else n_lat
    const = lambda shape: pl.BlockSpec(shape, lambda b, t: (0,) * len(shape),
                                       pipeline_mode=pl.Buffered(1))
    y_specs = [pl.BlockSpec((1, MIX_WIDTH, TM), lambda b, t: (b, 0, jnp.minimum(t, n_lat - 1)))]
    y_args = [yT]
    if has_ctx:
        y_specs.append(pl.BlockSpec((1, MIX_WIDTH, TM), lambda b, t: (b, 0, 0)))
        y_args.append(yT_ctx)
    return pl.pallas_call(
        functools.partial(_out_ffn_kernel, n_lat_tiles=n_lat, has_ctx=has_ctx),
        grid=(bsz, n_tiles),
        in_specs=y_specs + [
            pl.BlockSpec((1, TM, D_MODEL), lambda b, t: (b, t, 0)),
            pl.BlockSpec((1, 1, 6 * D_MODEL), lambda b, t: (jnp.where(t == nt_all - 1, 2, b), 0, 0)),
            const((1, D_MODEL)),
            const((1, D_MODEL)),
            const((1, D_MODEL)),
            const((MIX_WIDTH, D_MODEL)),
            const((D_MODEL, FFN_HIDDEN)),
            const((D_MODEL, FFN_HIDDEN)),
            const((FFN_HIDDEN, D_MODEL)),
        ],
        out_specs=pl.BlockSpec((1, TM, D_MODEL), lambda b, t: (b, t, 0)),
        out_shape=jax.ShapeDtypeStruct((bsz, n_tiles * TM, D_MODEL), F32),
        compiler_params=pltpu.CompilerParams(
            dimension_semantics=("parallel", "parallel"),
            vmem_limit_bytes=VMEM_LIMIT_BYTES),
        name="out_ffn",
    )(*y_args, x_all, mod3, lw["g_attn_post"], lw["g_ffn_pre"], lw["g_ffn_post"],
      lw["w_out"], lw["w_gate"], lw["w_up"], lw["w_down"])


def _rot_cols(w, half):
    n = w.shape[-1]
    wg = w.reshape(w.shape[:-1] + (n // (2 * half), 2, half))
    return jnp.concatenate([-wg[..., 1:2, :], wg[..., 0:1, :]], axis=-2).reshape(w.shape)


def _swap_halves(g, half):
    return jnp.concatenate([g[half:], g[:half]])


def _rope_tables():
    t = jnp.arange(SEQ, dtype=jnp.int32)
    row = (t // GRID_W).astype(F32)
    col = (t % GRID_W).astype(F32)

    def tables(rot_dim):
        quarter = rot_dim // 4
        inv = ROPE_THETA ** (-jnp.arange(quarter, dtype=F32) / quarter)
        ang = jnp.concatenate([row[:, None] * inv, col[:, None] * inv], axis=-1)
        cos = jnp.concatenate([jnp.cos(ang), jnp.ones((CTX_LEN, rot_dim // 2), F32)], axis=0)
        sin = jnp.concatenate([jnp.sin(ang), jnp.zeros((CTX_LEN, rot_dim // 2), F32)], axis=0)
        return jnp.tile(cos, (1, 2)), jnp.tile(sin, (1, 2))

    c_s, s_s = tables(MLA_ROPE)
    c_l, s_l = tables(GQA_DIM)
    return {
        "cS": jnp.tile(c_s, (1, 8)), "sS": jnp.tile(s_s, (1, 8)),
        "cL": jnp.tile(c_l, (1, 2)), "sL": jnp.tile(s_l, (1, 2)),
        "cST": c_s.T, "sST": s_s.T,
        "cLT": c_l.T, "sLT": s_l.T,
    }


def _layer_weights(l, w_in, g_attn_pre, g_attn_post, g_mla_q, w_mla_qb, g_mla_kv, w_mla_kvb,
                   g_diff_sub, g_gqa_q, g_gqa_k, w_out, g_ffn_pre, g_ffn_post, w_ffn_gate,
                   w_ffn_up, w_ffn_down, lambdas):
    wi = w_in[l]
    q_a = wi[:, 0:256]
    c_kv = wi[:, 256:384]
    k_pe = wi[:, 384:416]
    dq = wi[:, 416:672]
    dk = wi[:, 672:928]
    dv = wi[:, 928:1184]
    gq = wi[:, 1184:1696]
    gk = wi[:, 1696:1824]
    gv = wi[:, 1824:1952]
    zeros = lambda n: jnp.zeros((D_MODEL, n), F32)
    pe_slab = jnp.concatenate([zeros(64), k_pe, zeros(32)], axis=1)
    pe_rot_slab = jnp.concatenate([zeros(64), _rot_cols(k_pe, 16), zeros(32)], axis=1)
    wk = jnp.concatenate([c_kv, pe_slab, pe_rot_slab, dk, _rot_cols(dk, 16), gk, _rot_cols(gk, 32)],
                         axis=1)
    wq = jnp.concatenate([q_a, c_kv, dq, _rot_cols(dq, 16), gq, _rot_cols(gq, 32), dv, gv], axis=1)
    qb = w_mla_qb[l].reshape(MLA_Q_RANK, MLA_HEADS, MLA_NOPE + MLA_ROPE)
    qb_pe = qb[:, :, MLA_NOPE:]
    wqb = jnp.concatenate([qb, _rot_cols(qb_pe, 16)], axis=-1).reshape(MLA_Q_RANK, 512)
    kvb = w_mla_kvb[l].reshape(MLA_KV_RANK, MLA_HEADS, MLA_NOPE + MLA_V)
    wkvk = jnp.concatenate([kvb[:, :, :MLA_NOPE], jnp.zeros((MLA_KV_RANK, MLA_HEADS, 64), F32)],
                           axis=-1).reshape(MLA_KV_RANK, 512)
    wkvv = kvb[:, :, MLA_NOPE:].reshape(MLA_KV_RANK, MLA_HEADS * MLA_V)
    bcast = lambda g, n: jnp.broadcast_to(g[:, None], (g.shape[0], n))
    return {
        "g_attn_pre": g_attn_pre[l][None], "g_attn_post": g_attn_post[l][None],
        "g_ffn_pre": g_ffn_pre[l][None], "g_ffn_post": g_ffn_post[l][None],
        "wk": wk.astype(BF16), "wqT": wq.T.astype(BF16), "wqbT": wqb.T.astype(BF16),
        "wkvk": wkvk.astype(BF16), "wkvvT": wkvv.T.astype(BF16),
        "gqmla_b": bcast(g_mla_q[l], TM), "gkv_tok": g_mla_kv[l][None],
        "gkvT_b": bcast(g_mla_kv[l], TM),
        "ggq_b": bcast(g_gqa_q[l], TM), "ggqrot_b": bcast(_swap_halves(g_gqa_q[l], 32), TM),
        "ggk128": jnp.tile(g_gqa_k[l], 2)[None],
        "ggkrot128": jnp.tile(_swap_halves(g_gqa_k[l], 32), 2)[None],
        "gsub_b": bcast(g_diff_sub[l], TQ),
        "lam_vecs": jnp.stack([lam[l] for lam in lambdas]),
        "w_out": w_out[l].astype(BF16), "w_gate": w_ffn_gate[l].astype(BF16),
        "w_up": w_ffn_up[l].astype(BF16), "w_down": w_ffn_down[l].astype(BF16),
    }


def kernel(x, c, ctx, c_ctx, w_ada, b_ada, g_attn_pre, g_attn_post, w_in, g_mla_q, w_mla_qb, g_mla_kv, w_mla_kvb, lambda_q1, lambda_k1, lambda_q2, lambda_k2, g_diff_sub, g_gqa_q, g_gqa_k, w_out, g_ffn_pre, g_ffn_post, w_ffn_gate, w_ffn_up, w_ffn_down):
    bsz = x.shape[0]
    depth = w_ada.shape[0]
    assert x.shape == (bsz, SEQ, D_MODEL) and ctx.shape == (bsz, CTX_LEN, D_MODEL) and bsz == 2
    tabs = _rope_tables()
    cvec = jnp.concatenate([c, c_ctx[None], jnp.zeros((8 - bsz - 1, D_MODEL), F32)], axis=0)
    mod_all = _adaln(cvec, w_ada, b_ada)
    x_all = jnp.concatenate([x, ctx], axis=1)
    for l in range(depth):
        last = l == depth - 1
        lam_init = 0.8 - 0.6 * math.exp(-0.3 * l)
        lw = _layer_weights(l, w_in, g_attn_pre, g_attn_post, g_mla_q, w_mla_qb, g_mla_kv,
                            w_mla_kvb, g_diff_sub, g_gqa_q, g_gqa_k, w_out, g_ffn_pre, g_ffn_post,
                            w_ffn_gate, w_ffn_up, w_ffn_down,
                            (lambda_q1, lambda_k1, lambda_q2, lambda_k2))
        mod3 = mod_all[l].reshape(8, 1, 6 * D_MODEL)
        qT, kslab, vT = _proj(x_all, mod3, lw, tabs)
        yT = _attn(qT, kslab, vT, lw["lam_vecs"], lw["gsub_b"], lam_init, tq=TQ, tk=TK,
                   q_blocks=SEQ // TQ, q_off=0, k_blocks=T_ALL // TK, k_off=0, name="attn")
        yT_ctx = None
        if not last:
            yT_ctx = _attn(qT, kslab, vT, lw["lam_vecs"], lw["gsub_b"][:, :CTX_LEN], lam_init,
                           tq=CTX_LEN, tk=CTX_LEN, q_blocks=1, q_off=SEQ // CTX_LEN, k_blocks=1,
                           k_off=SEQ // CTX_LEN, name="attn_ctx")
        x_all = _out_ffn(yT, yT_ctx, x_all, mod3, lw)
    return x_all
```

```python
import functools
import math

import jax
import jax.numpy as jnp
from jax import lax
from jax.experimental import pallas as pl
from jax.experimental.pallas import tpu as pltpu

F32 = jnp.float32
BF16 = jnp.bfloat16

D_MODEL = 1024
SEQ = 8192
GRID_W = 64
CTX_LEN = 256
T_ALL = SEQ + CTX_LEN
ROPE_THETA = 10000.0
EPS = 1e-6

MLA_HEADS = 4
MLA_Q_RANK = 256
MLA_KV_RANK = 128
MLA_NOPE = 64
MLA_ROPE = 32
MLA_V = 64
DIFF_HEADS = 4
DIFF_QK = 32
DIFF_V = 64
GQA_HEADS = 8
GQA_KV_HEADS = 2
GQA_DIM = 64
HEAD_V = 64
MIX_WIDTH = 1024
FFN_HIDDEN = 2816

LOG2E = 1.4426950408889634
MLA_QS = LOG2E / math.sqrt(MLA_NOPE + MLA_ROPE)
DIFF_QS = LOG2E / math.sqrt(DIFF_QK)
GQA_QS = LOG2E / math.sqrt(GQA_DIM)

LANES = 128
BF16_SUBLANES = 16
VMEM_LIMIT_BYTES = 56 * 1024 * 1024

N_SCORE_HEADS = 20
N_KEY_SLABS = 7
N_VALUE_HEADS = 10
V_ROWS = HEAD_V + BF16_SUBLANES
KEY_SLAB_OF_HEAD = (0, 1, 2, 3, 4, 4, 4, 4, 5, 5, 5, 5, 6, 6, 6, 6, 6, 6, 6, 6)
VALUE_HEAD_OF_HEAD = (0, 1, 2, 3, 4, 4, 5, 5, 6, 6, 7, 7, 8, 8, 8, 8, 9, 9, 9, 9)

TM = 256
TQ = 512
TK = 768
KEY_CHUNK = 256
STALE_MAX_SLACK = 64.0

WK_COLS = 1152
WQ_ROWS = 2304


def _rms_rows(x, eps=EPS):
    return lax.rsqrt(jnp.mean(x * x, axis=-1, keepdims=True) + eps)


def _rms_cols(x, eps=EPS):
    return lax.rsqrt(jnp.mean(x * x, axis=0, keepdims=True) + eps)


def _adaln_kernel(c_ref, w_ref, b_ref, o_ref):
    c = c_ref[...]
    sc = c * jax.nn.sigmoid(c)
    o_ref[0] = jnp.dot(sc.astype(BF16), w_ref[0].astype(BF16),
                       preferred_element_type=F32) + b_ref[0]


def _adaln(cvec, w_ada, b_ada):
    depth = w_ada.shape[0]
    tn = 1024
    n_out = w_ada.shape[2]
    return pl.pallas_call(
        _adaln_kernel,
        grid=(depth, n_out // tn),
        in_specs=[
            pl.BlockSpec((8, D_MODEL), lambda l, j: (0, 0)),
            pl.BlockSpec((1, D_MODEL, tn), lambda l, j: (l, 0, j)),
            pl.BlockSpec((1, 1, tn), lambda l, j: (l, 0, j)),
        ],
        out_specs=pl.BlockSpec((1, 8, tn), lambda l, j: (l, 0, j)),
        out_shape=jax.ShapeDtypeStruct((depth, 8, n_out), F32),
        compiler_params=pltpu.CompilerParams(
            dimension_semantics=("arbitrary", "arbitrary"),
            vmem_limit_bytes=VMEM_LIMIT_BYTES),
        name="adaln",
    )(cvec, w_ada, b_ada.reshape(depth, 1, n_out))


def _proj_kernel(x_ref, mod_ref, gpre_ref, wk_ref, wqT_ref, wqbT_ref, wkvk_ref, wkvvT_ref,
                 gqmla_ref, gkv_tok_ref, gkvT_ref, ggq_ref, ggqrot_ref, ggk_ref, ggkrot_ref,
                 cS_ref, sS_ref, cL_ref, sL_ref, cST_ref, sST_ref, cLT_ref, sLT_ref,
                 qT_ref, k_ref, vT_ref):
    tm = x_ref.shape[1]
    x = x_ref[0]
    mod = mod_ref[0]
    sh = mod[:, 0:D_MODEL]
    sc = mod[:, D_MODEL:2 * D_MODEL]
    h = (x * _rms_rows(x) * gpre_ref[...]) * (1.0 + sc) + sh
    hb = h.astype(BF16)
    pk = jnp.dot(hb, wk_ref[...], preferred_element_type=F32)
    pT = lax.dot_general(wqT_ref[...], hb, (((1,), (1,)), ((), ())),
                         preferred_element_type=F32)

    cS = cS_ref[...]
    sS = sS_ref[...]
    cL = cL_ref[...]
    sL = sL_ref[...]
    cST = cST_ref[...]
    sST = sST_ref[...]
    cLT = cLT_ref[...]
    sLT = sLT_ref[...]
    ones_rows = jnp.ones((BF16_SUBLANES, tm), BF16)
    zeros32 = jnp.zeros((32, tm), BF16)
    zeros64 = jnp.zeros((64, tm), BF16)

    qaT = pT[0:256]
    qn = (qaT * _rms_cols(qaT) * gqmla_ref[...]).astype(BF16)
    qm = jnp.dot(wqbT_ref[...], qn, preferred_element_type=F32)
    for hh in range(MLA_HEADS):
        blk = qm[128 * hh:128 * (hh + 1)]
        pe = blk[64:96] * cST + blk[96:128] * sST
        qT_ref[0, hh, 0:64, :] = (blk[0:64] * MLA_QS).astype(BF16)
        qT_ref[0, hh, 64:96, :] = (pe * MLA_QS).astype(BF16)
        qT_ref[0, hh, 96:128, :] = zeros32

    ckv = pk[:, 0:128]
    cn = (ckv * _rms_rows(ckv) * gkv_tok_ref[...]).astype(BF16)
    kn = jnp.dot(cn, wkvk_ref[...], preferred_element_type=F32)
    pe_tok = pk[:, 128:256] * cS[:, 0:128] + pk[:, 256:384] * sS[:, 0:128]
    for hh in range(MLA_HEADS):
        k_ref[0, :, 128 * hh:128 * (hh + 1)] = (kn[:, 128 * hh:128 * (hh + 1)] + pe_tok).astype(BF16)

    ckvT = pT[256:384]
    cnT = (ckvT * _rms_cols(ckvT) * gkvT_ref[...]).astype(BF16)
    vmT = jnp.dot(wkvvT_ref[...], cnT, preferred_element_type=F32)
    for hh in range(MLA_HEADS):
        vT_ref[0, hh, 0:64, :] = vmT[64 * hh:64 * (hh + 1)].astype(BF16)
        vT_ref[0, hh, 64:V_ROWS, :] = ones_rows

    qd = pT[384:640].reshape(8, 32, tm)
    qdr = pT[640:896].reshape(8, 32, tm)
    qd = (qd * cST[None] + qdr * sST[None]) * DIFF_QS
    for j in range(8):
        for rb in range(4):
            val = qd[j].astype(BF16) if rb == j % 4 else zeros32
            qT_ref[0, 4 + j, 32 * rb:32 * (rb + 1), :] = val
    kd = pk[:, 384:640] * cS + pk[:, 640:896] * sS
    k_ref[0, :, 512:768] = kd.astype(BF16)
    vdT = pT[1920:2176]
    for hh in range(DIFF_HEADS):
        vT_ref[0, 4 + hh, 0:64, :] = vdT[64 * hh:64 * (hh + 1)].astype(BF16)
        vT_ref[0, 4 + hh, 64:V_ROWS, :] = ones_rows

    gq = pT[896:1408].reshape(8, 64, tm)
    gqr = pT[1408:1920].reshape(8, 64, tm)
    rq = lax.rsqrt(jnp.mean(gq * gq, axis=1, keepdims=True) + EPS)
    qg = ((gq * rq * ggq_ref[...][None]) * cLT[None]
          + (gqr * rq * ggqrot_ref[...][None]) * sLT[None]) * GQA_QS
    for j in range(8):
        grp = j // 4
        for rb in range(2):
            val = qg[j].astype(BF16) if rb == grp else zeros64
            qT_ref[0, 12 + j, 64 * rb:64 * (rb + 1), :] = val
    gk = pk[:, 896:1024]
    gkr = pk[:, 1024:1152]
    sq = gk * gk
    lane = lax.broadcasted_iota(jnp.int32, sq.shape, 1)
    lo = lane < GQA_DIM
    s0 = jnp.sum(jnp.where(lo, sq, 0.0), axis=-1, keepdims=True)
    s1 = jnp.sum(jnp.where(lo, 0.0, sq), axis=-1, keepdims=True)
    rk = jnp.where(lo, lax.rsqrt(s0 / GQA_DIM + EPS), lax.rsqrt(s1 / GQA_DIM + EPS))
    kg = (gk * rk * ggk_ref[...]) * cL + (gkr * rk * ggkrot_ref[...]) * sL
    k_ref[0, :, 768:896] = kg.astype(BF16)
    gvT = pT[2176:2304]
    for hh in range(GQA_KV_HEADS):
        vT_ref[0, 8 + hh, 0:64, :] = gvT[64 * hh:64 * (hh + 1)].astype(BF16)
        vT_ref[0, 8 + hh, 64:V_ROWS, :] = ones_rows


def _proj(x_all, mod3, lw, tabs):
    bsz, t_all, _ = x_all.shape
    nt = t_all // TM
    const = lambda shape: pl.BlockSpec(shape, lambda b, t: (0,) * len(shape))
    in_specs = [
        pl.BlockSpec((1, TM, D_MODEL), lambda b, t: (b, t, 0)),
        pl.BlockSpec((1, 1, 6 * D_MODEL), lambda b, t: (jnp.where(t == nt - 1, 2, b), 0, 0)),
        const((1, D_MODEL)),
        const((D_MODEL, WK_COLS)),
        const((WQ_ROWS, D_MODEL)),
        const((512, 256)),
        const((128, 512)),
        const((256, 128)),
        const((256, TM)),
        const((1, 128)),
        const((128, TM)),
        const((64, TM)),
        const((64, TM)),
        const((1, 128)),
        const((1, 128)),
        pl.BlockSpec((TM, 256), lambda b, t: (t, 0)),
        pl.BlockSpec((TM, 256), lambda b, t: (t, 0)),
        pl.BlockSpec((TM, 128), lambda b, t: (t, 0)),
        pl.BlockSpec((TM, 128), lambda b, t: (t, 0)),
        pl.BlockSpec((32, TM), lambda b, t: (0, t)),
        pl.BlockSpec((32, TM), lambda b, t: (0, t)),
        pl.BlockSpec((64, TM), lambda b, t: (0, t)),
        pl.BlockSpec((64, TM), lambda b, t: (0, t)),
    ]
    out_specs = [
        pl.BlockSpec((1, N_SCORE_HEADS, 128, TM), lambda b, t: (b, 0, 0, t)),
        pl.BlockSpec((1, TM, N_KEY_SLABS * 128), lambda b, t: (b, t, 0)),
        pl.BlockSpec((1, N_VALUE_HEADS, V_ROWS, TM), lambda b, t: (b, 0, 0, t)),
    ]
    out_shape = [
        jax.ShapeDtypeStruct((bsz, N_SCORE_HEADS, 128, t_all), BF16),
        jax.ShapeDtypeStruct((bsz, t_all, N_KEY_SLABS * 128), BF16),
        jax.ShapeDtypeStruct((bsz, N_VALUE_HEADS, V_ROWS, t_all), BF16),
    ]
    return pl.pallas_call(
        _proj_kernel,
        grid=(bsz, nt),
        in_specs=in_specs,
        out_specs=out_specs,
        out_shape=out_shape,
        compiler_params=pltpu.CompilerParams(
            dimension_semantics=("parallel", "parallel"),
            vmem_limit_bytes=VMEM_LIMIT_BYTES),
        name="proj",
    )(x_all, mod3, lw["g_attn_pre"], lw["wk"], lw["wqT"], lw["wqbT"], lw["wkvk"], lw["wkvvT"],
      lw["gqmla_b"], lw["gkv_tok"], lw["gkvT_b"], lw["ggq_b"], lw["ggqrot_b"], lw["ggk128"],
      lw["ggkrot128"],
      tabs["cS"], tabs["sS"], tabs["cL"], tabs["sL"], tabs["cST"], tabs["sST"], tabs["cLT"],
      tabs["sLT"])


def _attn_kernel(qT_ref, k_ref, vT_ref, lam_ref, gsub_ref, yT_ref, acc_ref, m_ref, kept_ref, *,
                 n_k, lam_init):
    ki = pl.program_id(2)
    tk = k_ref.shape[1]
    n_chunks = tk // KEY_CHUNK

    def scores(h, c):
        slab = KEY_SLAB_OF_HEAD[h]
        ks = k_ref[0, KEY_CHUNK * c:KEY_CHUNK * (c + 1), 128 * slab:128 * (slab + 1)]
        return jnp.dot(ks, qT_ref[0, h], preferred_element_type=F32)

    def weighted_values(h, c, pT):
        v = vT_ref[0, VALUE_HEAD_OF_HEAD[h], :, KEY_CHUNK * c:KEY_CHUNK * (c + 1)]
        return jnp.dot(v, pT, preferred_element_type=F32)

    def column_max(m, s_chunks):
        for s in s_chunks:
            m = jnp.maximum(m, jnp.max(s, axis=0, keepdims=True))
        return m

    def sweep_heads(lookahead, head_step):
        pending = [[scores(h, c) for c in range(n_chunks)] for h in range(lookahead)]
        for h in range(N_SCORE_HEADS):
            s_chunks = pending.pop(0)
            issued = []

            def issue_ahead(c, h=h, issued=issued):
                if h + lookahead < N_SCORE_HEADS:
                    issued.append(scores(h + lookahead, c))

            head_step(h, s_chunks, issue_ahead)
            if issued:
                pending.append(issued)

    @pl.when(ki == 0)
    def _first_tile():
        def head_step(h, s_chunks, issue_ahead):
            m_new = column_max(jnp.full((1, s_chunks[0].shape[1]), -jnp.inf, F32), s_chunks)
            oT = None
            for c, s in enumerate(s_chunks):
                issue_ahead(c)
                o_c = weighted_values(h, c, jnp.exp2(s - m_new).astype(BF16))
                oT = o_c if oT is None else oT + o_c
            acc_ref[h] = oT
            m_ref[h] = m_new

        sweep_heads(2, head_step)

    if n_k > 1:
        @pl.when(ki > 0)
        def _later_tiles():
            excess = []

            def head_step(h, s_chunks, issue_ahead):
                m_used = m_ref[h]
                m_tile = m_used
                oT = None
                for c, s in enumerate(s_chunks):
                    m_tile = jnp.maximum(m_tile, jnp.max(s, axis=0, keepdims=True))
                    o_c = weighted_values(h, c, jnp.exp2(s - m_used).astype(BF16))
                    issue_ahead(c)
                    oT = o_c if oT is None else oT + o_c
                acc_old = acc_ref[h]
                over = m_tile - m_used
                keep_old = over > STALE_MAX_SLACK
                acc_ref[h] = jnp.where(keep_old, acc_old, (acc_old + oT) * jnp.exp2(-over))
                m_ref[h] = jnp.where(keep_old, m_used, m_tile)
                kept_ref[h] = jnp.where(keep_old, 1.0, 0.0)
                excess.append(over)

            sweep_heads(2, head_step)
            worst = excess[0]
            for e in excess[1:]:
                worst = jnp.maximum(worst, e)

            @pl.when(jnp.max(worst) > STALE_MAX_SLACK)
            def _redo_kept_columns():
                def redo_head(h, carry):
                    slab = jnp.where(h < 4, h, jnp.where(h < 12, 4 + ((h - 4) >> 2), 6))
                    vh = jnp.where(h < 4, h, jnp.where(h < 12, 4 + ((h - 4) >> 1),
                                                       8 + ((h - 12) >> 2)))
                    ks = k_ref[0, :, pl.ds(pl.multiple_of(slab * 128, 128), 128)]
                    sT = jnp.dot(ks, qT_ref[0, h], preferred_element_type=F32)
                    m_old = m_ref[h]
                    m_new = jnp.maximum(m_old, jnp.max(sT, axis=0, keepdims=True))
                    kept = kept_ref[h] > 0.5
                    oT = jnp.dot(vT_ref[0, vh], jnp.exp2(sT - m_new).astype(BF16),
                                 preferred_element_type=F32)
                    acc_old = acc_ref[h]
                    acc_ref[h] = jnp.where(kept, acc_old * jnp.exp2(m_old - m_new) + oT, acc_old)
                    m_ref[h] = jnp.where(kept, m_new, m_old)
                    return carry

                lax.fori_loop(0, N_SCORE_HEADS, redo_head, 0)

    @pl.when(ki == n_k - 1)
    def _finalize():
        def head_out(h):
            a = acc_ref[h]
            return a[0:HEAD_V] / a[HEAD_V:HEAD_V + 1]
        for hh in range(MLA_HEADS):
            yT_ref[0, 64 * hh:64 * (hh + 1), :] = head_out(hh).astype(BF16)
        lp = lam_ref[...]
        l1 = jnp.sum(lp[0:1] * lp[1:2], axis=-1, keepdims=True)
        l2 = jnp.sum(lp[2:3] * lp[3:4], axis=-1, keepdims=True)
        lam = jnp.exp(l1) - jnp.exp(l2) + lam_init
        gsub = gsub_ref[...]
        for hh in range(DIFF_HEADS):
            d = head_out(4 + 2 * hh) - lam * head_out(5 + 2 * hh)
            y = (d * _rms_cols(d) * gsub) * (1.0 - lam_init)
            yT_ref[0, 256 + 64 * hh:256 + 64 * (hh + 1), :] = y.astype(BF16)
        for j in range(GQA_HEADS):
            yT_ref[0, 512 + 64 * j:512 + 64 * (j + 1), :] = head_out(12 + j).astype(BF16)


def _attn(qT, kslab, vT, lam_vecs, gsub_b, lam_init, *, tq, tk, q_blocks, q_off, k_blocks, k_off,
          name):
    bsz = qT.shape[0]
    return pl.pallas_call(
        functools.partial(_attn_kernel, n_k=k_blocks, lam_init=lam_init),
        grid=(bsz, q_blocks, k_blocks),
        in_specs=[
            pl.BlockSpec((1, N_SCORE_HEADS, 128, tq), lambda b, qi, ki: (b, 0, 0, qi + q_off)),
            pl.BlockSpec((1, tk, N_KEY_SLABS * 128), lambda b, qi, ki: (b, ki + k_off, 0)),
            pl.BlockSpec((1, N_VALUE_HEADS, V_ROWS, tk), lambda b, qi, ki: (b, 0, 0, ki + k_off)),
            pl.BlockSpec((4, DIFF_QK), lambda b, qi, ki: (0, 0)),
            pl.BlockSpec((HEAD_V, tq), lambda b, qi, ki: (0, 0)),
        ],
        out_specs=pl.BlockSpec((1, MIX_WIDTH, tq), lambda b, qi, ki: (b, 0, qi)),
        out_shape=jax.ShapeDtypeStruct((bsz, MIX_WIDTH, q_blocks * tq), BF16),
        scratch_shapes=[
            pltpu.VMEM((N_SCORE_HEADS, V_ROWS, tq), F32),
            pltpu.VMEM((N_SCORE_HEADS, 1, tq), F32),
            pltpu.VMEM((N_SCORE_HEADS, 1, tq), F32),
        ],
        compiler_params=pltpu.CompilerParams(
            dimension_semantics=("parallel", "parallel", "arbitrary"),
            vmem_limit_bytes=VMEM_LIMIT_BYTES),
        name=name,
    )(qT, kslab, vT, lam_vecs, gsub_b)


def _out_ffn_kernel(*refs, n_lat_tiles, has_ctx):
    if has_ctx:
        yT_ref, yTc_ref = refs[:2]
        refs = refs[2:]
        yT = jnp.where(pl.program_id(1) == n_lat_tiles, yTc_ref[0], yT_ref[0])
    else:
        yT = refs[0][0]
        refs = refs[1:]
    x_ref, mod_ref, gpost_ref, gfpre_ref, gfpost_ref, wout_ref, wg_ref, wu_ref, wd_ref, o_ref = refs
    yp = lax.dot_general(yT, wout_ref[...], (((0,), (0,)), ((), ())),
                         preferred_element_type=F32)
    mod = mod_ref[0]
    gt_a = mod[:, 2 * D_MODEL:3 * D_MODEL]
    sh_f = mod[:, 3 * D_MODEL:4 * D_MODEL]
    sc_f = mod[:, 4 * D_MODEL:5 * D_MODEL]
    gt_f = mod[:, 5 * D_MODEL:6 * D_MODEL]
    x1 = x_ref[0] + gt_a * (yp * _rms_rows(yp) * gpost_ref[...])
    hf = (x1 * _rms_rows(x1) * gfpre_ref[...]) * (1.0 + sc_f) + sh_f
    hb = hf.astype(BF16)
    g = jnp.dot(hb, wg_ref[...], preferred_element_type=F32)
    u = jnp.dot(hb, wu_ref[...], preferred_element_type=F32)
    a = (g * jax.nn.sigmoid(g)) * u
    f = jnp.dot(a.astype(BF16), wd_ref[...], preferred_element_type=F32)
    o_ref[0] = x1 + gt_f * (f * _rms_rows(f) * gfpost_ref[...])


def _out_ffn(yT, yT_ctx, x_all, mod3, lw):
    bsz = x_all.shape[0]
    nt_all = x_all.shape[1] // TM
    n_lat = SEQ // TM
    has_ctx = yT_ctx is not None
    n_tiles = n_lat + 1 if has_ctx else n_lat
    const = lambda shape: pl.BlockSpec(shape, lambda b, t: (0,) * len(shape),
                                       pipeline_mode=pl.Buffered(1))
    y_specs = [pl.BlockSpec((1, MIX_WIDTH, TM), lambda b, t: (b, 0, jnp.minimum(t, n_lat - 1)))]
    y_args = [yT]
    if has_ctx:
        y_specs.append(pl.BlockSpec((1, MIX_WIDTH, TM), lambda b, t: (b, 0, 0)))
        y_args.append(yT_ctx)
    return pl.pallas_call(
        functools.partial(_out_ffn_kernel, n_lat_tiles=n_lat, has_ctx=has_ctx),
        grid=(bsz, n_tiles),
        in_specs=y_specs + [
            pl.BlockSpec((1, TM, D_MODEL), lambda b, t: (b, t, 0)),
            pl.BlockSpec((1, 1, 6 * D_MODEL), lambda b, t: (jnp.where(t == nt_all - 1, 2, b), 0, 0)),
            const((1, D_MODEL)),
            const((1, D_MODEL)),
            const((1, D_MODEL)),
            const((MIX_WIDTH, D_MODEL)),
            const((D_MODEL, FFN_HIDDEN)),
            const((D_MODEL, FFN_HIDDEN)),
            const((FFN_HIDDEN, D_MODEL)),
        ],
        out_specs=pl.BlockSpec((1, TM, D_MODEL), lambda b, t: (b, t, 0)),
        out_shape=jax.ShapeDtypeStruct((bsz, n_tiles * TM, D_MODEL), F32),
        compiler_params=pltpu.CompilerParams(
            dimension_semantics=("parallel", "parallel"),
            vmem_limit_bytes=VMEM_LIMIT_BYTES),
        name="out_ffn",
    )(*y_args, x_all, mod3, lw["g_attn_post"], lw["g_ffn_pre"], lw["g_ffn_post"],
      lw["w_out"], lw["w_gate"], lw["w_up"], lw["w_down"])


def _rot_cols(w, half):
    n = w.shape[-1]
    wg = w.reshape(w.shape[:-1] + (n // (2 * half), 2, half))
    return jnp.concatenate([-wg[..., 1:2, :], wg[..., 0:1, :]], axis=-2).reshape(w.shape)


def _swap_halves(g, half):
    return jnp.concatenate([g[half:], g[:half]])


def _rope_tables():
    t = jnp.arange(SEQ, dtype=jnp.int32)
    row = (t // GRID_W).astype(F32)
    col = (t % GRID_W).astype(F32)

    def tables(rot_dim):
        quarter = rot_dim // 4
        inv = ROPE_THETA ** (-jnp.arange(quarter, dtype=F32) / quarter)
        ang = jnp.concatenate([row[:, None] * inv, col[:, None] * inv], axis=-1)
        cos = jnp.concatenate([jnp.cos(ang), jnp.ones((CTX_LEN, rot_dim // 2), F32)], axis=0)
        sin = jnp.concatenate([jnp.sin(ang), jnp.zeros((CTX_LEN, rot_dim // 2), F32)], axis=0)
        return jnp.tile(cos, (1, 2)), jnp.tile(sin, (1, 2))

    c_s, s_s = tables(MLA_ROPE)
    c_l, s_l = tables(GQA_DIM)
    return {
        "cS": jnp.tile(c_s, (1, 8)), "sS": jnp.tile(s_s, (1, 8)),
        "cL": jnp.tile(c_l, (1, 2)), "sL": jnp.tile(s_l, (1, 2)),
        "cST": c_s.T, "sST": s_s.T,
        "cLT": c_l.T, "sLT": s_l.T,
    }


def _layer_weights(l, w_in, g_attn_pre, g_attn_post, g_mla_q, w_mla_qb, g_mla_kv, w_mla_kvb,
                   g_diff_sub, g_gqa_q, g_gqa_k, w_out, g_ffn_pre, g_ffn_post, w_ffn_gate,
                   w_ffn_up, w_ffn_down, lambdas):
    wi = w_in[l]
    q_a = wi[:, 0:256]
    c_kv = wi[:, 256:384]
    k_pe = wi[:, 384:416]
    dq = wi[:, 416:672]
    dk = wi[:, 672:928]
    dv = wi[:, 928:1184]
    gq = wi[:, 1184:1696]
    gk = wi[:, 1696:1824]
    gv = wi[:, 1824:1952]
    zeros = lambda n: jnp.zeros((D_MODEL, n), F32)
    pe_slab = jnp.concatenate([zeros(64), k_pe, zeros(32)], axis=1)
    pe_rot_slab = jnp.concatenate([zeros(64), _rot_cols(k_pe, 16), zeros(32)], axis=1)
    wk = jnp.concatenate([c_kv, pe_slab, pe_rot_slab, dk, _rot_cols(dk, 16), gk, _rot_cols(gk, 32)],
                         axis=1)
    wq = jnp.concatenate([q_a, c_kv, dq, _rot_cols(dq, 16), gq, _rot_cols(gq, 32), dv, gv], axis=1)
    qb = w_mla_qb[l].reshape(MLA_Q_RANK, MLA_HEADS, MLA_NOPE + MLA_ROPE)
    qb_pe = qb[:, :, MLA_NOPE:]
    wqb = jnp.concatenate([qb, _rot_cols(qb_pe, 16)], axis=-1).reshape(MLA_Q_RANK, 512)
    kvb = w_mla_kvb[l].reshape(MLA_KV_RANK, MLA_HEADS, MLA_NOPE + MLA_V)
    wkvk = jnp.concatenate([kvb[:, :, :MLA_NOPE], jnp.zeros((MLA_KV_RANK, MLA_HEADS, 64), F32)],
                           axis=-1).reshape(MLA_KV_RANK, 512)
    wkvv = kvb[:, :, MLA_NOPE:].reshape(MLA_KV_RANK, MLA_HEADS * MLA_V)
    bcast = lambda g, n: jnp.broadcast_to(g[:, None], (g.shape[0], n))
    return {
        "g_attn_pre": g_attn_pre[l][None], "g_attn_post": g_attn_post[l][None],
        "g_ffn_pre": g_ffn_pre[l][None], "g_ffn_post": g_ffn_post[l][None],
        "wk": wk.astype(BF16), "wqT": wq.T.astype(BF16), "wqbT": wqb.T.astype(BF16),
        "wkvk": wkvk.astype(BF16), "wkvvT": wkvv.T.astype(BF16),
        "gqmla_b": bcast(g_mla_q[l], TM), "gkv_tok": g_mla_kv[l][None],
        "gkvT_b": bcast(g_mla_kv[l], TM),
        "ggq_b": bcast(g_gqa_q[l], TM), "ggqrot_b": bcast(_swap_halves(g_gqa_q[l], 32), TM),
        "ggk128": jnp.tile(g_gqa_k[l], 2)[None],
        "ggkrot128": jnp.tile(_swap_halves(g_gqa_k[l], 32), 2)[None],
        "gsub_b": bcast(g_diff_sub[l], TQ),
        "lam_vecs": jnp.stack([lam[l] for lam in lambdas]),
        "w_out": w_out[l].astype(BF16), "w_gate": w_ffn_gate[l].astype(BF16),
        "w_up": w_ffn_up[l].astype(BF16), "w_down": w_ffn_down[l].astype(BF16),
    }


def kernel(x, c, ctx, c_ctx, w_ada, b_ada, g_attn_pre, g_attn_post, w_in, g_mla_q, w_mla_qb, g_mla_kv, w_mla_kvb, lambda_q1, lambda_k1, lambda_q2, lambda_k2, g_diff_sub, g_gqa_q, g_gqa_k, w_out, g_ffn_pre, g_ffn_post, w_ffn_gate, w_ffn_up, w_ffn_down):
    bsz = x.shape[0]
    depth = w_ada.shape[0]
    assert x.shape == (bsz, SEQ, D_MODEL) and ctx.shape == (bsz, CTX_LEN, D_MODEL) and bsz == 2
    tabs = _rope_tables()
    cvec = jnp.concatenate([c, c_ctx[None], jnp.zeros((8 - bsz - 1, D_MODEL), F32)], axis=0)
    mod_all = _adaln(cvec, w_ada, b_ada)
    x_all = jnp.concatenate([x, ctx], axis=1)
    for l in range(depth):
        last = l == depth - 1
        lam_init = 0.8 - 0.6 * math.exp(-0.3 * l)
        lw = _layer_weights(l, w_in, g_attn_pre, g_attn_post, g_mla_q, w_mla_qb, g_mla_kv,
                            w_mla_kvb, g_diff_sub, g_gqa_q, g_gqa_k, w_out, g_ffn_pre, g_ffn_post,
                            w_ffn_gate, w_ffn_up, w_ffn_down,
                            (lambda_q1, lambda_k1, lambda_q2, lambda_k2))
        mod3 = mod_all[l].reshape(8, 1, 6 * D_MODEL)
        qT, kslab, vT = _proj(x_all, mod3, lw, tabs)
        yT = _attn(qT, kslab, vT, lw["lam_vecs"], lw["gsub_b"], lam_init, tq=TQ, tk=TK,
                   q_blocks=SEQ // TQ, q_off=0, k_blocks=T_ALL // TK, k_off=0, name="attn")
        yT_ctx = None
        if not last:
            yT_ctx = _attn(qT, kslab, vT, lw["lam_vecs"], lw["gsub_b"][:, :CTX_LEN], lam_init,
                           tq=CTX_LEN, tk=CTX_LEN, q_blocks=1, q_off=SEQ // CTX_LEN, k_blocks=1,
                           k_off=SEQ // CTX_LEN, name="attn_ctx")
        x_all = _out_ffn(yT, yT_ctx, x_all, mod3, lw)
    return x_all
```

```python
import functools
import math

import jax
import jax.numpy as jnp
from jax import lax
from jax.experimental import pallas as pl
from jax.experimental.pallas import tpu as pltpu

F32 = jnp.float32
BF16 = jnp.bfloat16

D_MODEL = 1024
SEQ = 8192
GRID_W = 64
CTX_LEN = 256
T_ALL = SEQ + CTX_LEN
ROPE_THETA = 10000.0
EPS = 1e-6

MLA_HEADS = 4
MLA_Q_RANK = 256
MLA_KV_RANK = 128
MLA_NOPE = 64
MLA_ROPE = 32
MLA_V = 64
DIFF_HEADS = 4
DIFF_QK = 32
DIFF_V = 64
GQA_HEADS = 8
GQA_KV_HEADS = 2
GQA_DIM = 64
HEAD_V = 64
MIX_WIDTH = 1024
FFN_HIDDEN = 2816

LOG2E = 1.4426950408889634
MLA_QS = LOG2E / math.sqrt(MLA_NOPE + MLA_ROPE)
DIFF_QS = LOG2E / math.sqrt(DIFF_QK)
GQA_QS = LOG2E / math.sqrt(GQA_DIM)

LANES = 128
BF16_SUBLANES = 16
VMEM_LIMIT_BYTES = 56 * 1024 * 1024

N_SCORE_HEADS = 20
N_KEY_SLABS = 7
N_VALUE_HEADS = 10
V_ROWS = HEAD_V + BF16_SUBLANES
KEY_SLAB_OF_HEAD = (0, 1, 2, 3, 4, 4, 4, 4, 5, 5, 5, 5, 6, 6, 6, 6, 6, 6, 6, 6)
VALUE_HEAD_OF_HEAD = (0, 1, 2, 3, 4, 4, 5, 5, 6, 6, 7, 7, 8, 8, 8, 8, 9, 9, 9, 9)

TM = 256
TQ = 512
TK = 768
KEY_CHUNK = 256
STALE_MAX_SLACK = 64.0

WK_COLS = 1152
WQ_ROWS = 2304


def _rms_rows(x, eps=EPS):
    return lax.rsqrt(jnp.mean(x * x, axis=-1, keepdims=True) + eps)


def _rms_cols(x, eps=EPS):
    return lax.rsqrt(jnp.mean(x * x, axis=0, keepdims=True) + eps)


def _adaln_kernel(c_ref, w_ref, b_ref, o_ref):
    c = c_ref[...]
    sc = c * jax.nn.sigmoid(c)
    o_ref[0] = jnp.dot(sc.astype(BF16), w_ref[0].astype(BF16),
                       preferred_element_type=F32) + b_ref[0]


def _adaln(cvec, w_ada, b_ada):
    depth = w_ada.shape[0]
    tn = 1024
    n_out = w_ada.shape[2]
    return pl.pallas_call(
        _adaln_kernel,
        grid=(depth, n_out // tn),
        in_specs=[
            pl.BlockSpec((8, D_MODEL), lambda l, j: (0, 0)),
            pl.BlockSpec((1, D_MODEL, tn), lambda l, j: (l, 0, j)),
            pl.BlockSpec((1, 1, tn), lambda l, j: (l, 0, j)),
        ],
        out_specs=pl.BlockSpec((1, 8, tn), lambda l, j: (l, 0, j)),
        out_shape=jax.ShapeDtypeStruct((depth, 8, n_out), F32),
        compiler_params=pltpu.CompilerParams(
            dimension_semantics=("arbitrary", "arbitrary"),
            vmem_limit_bytes=VMEM_LIMIT_BYTES),
        name="adaln",
    )(cvec, w_ada, b_ada.reshape(depth, 1, n_out))


def _token_tile(refs, n_parts):
    if n_parts == 1:
        return refs[0][0]
    return jnp.where(pl.program_id(1) == SEQ // TM, refs[1][0], refs[0][0])


def _proj_kernel(*refs, n_x_parts):
    x = _token_tile(refs, n_x_parts)
    (mod_ref, gpre_ref, wk_ref, wqT_ref, wqbT_ref, wkvk_ref, wkvvT_ref,
     gqmla_ref, gkv_tok_ref, gkvT_ref, ggq_ref, ggqrot_ref, ggk_ref, ggkrot_ref,
     cS_ref, sS_ref, cL_ref, sL_ref, cST_ref, sST_ref, cLT_ref, sLT_ref,
     qT_ref, k_ref, vT_ref) = refs[n_x_parts:]
    tm = x.shape[0]
    mod = mod_ref[0]
    sh = mod[:, 0:D_MODEL]
    sc = mod[:, D_MODEL:2 * D_MODEL]
    h = (x * _rms_rows(x) * gpre_ref[...]) * (1.0 + sc) + sh
    hb = h.astype(BF16)
    pk = jnp.dot(hb, wk_ref[...], preferred_element_type=F32)
    pT = lax.dot_general(wqT_ref[...], hb, (((1,), (1,)), ((), ())),
                         preferred_element_type=F32)

    cS = cS_ref[...]
    sS = sS_ref[...]
    cL = cL_ref[...]
    sL = sL_ref[...]
    cST = cST_ref[...]
    sST = sST_ref[...]
    cLT = cLT_ref[...]
    sLT = sLT_ref[...]
    ones_rows = jnp.ones((BF16_SUBLANES, tm), BF16)
    zeros32 = jnp.zeros((32, tm), BF16)
    zeros64 = jnp.zeros((64, tm), BF16)

    qaT = pT[0:256]
    qn = (qaT * _rms_cols(qaT) * gqmla_ref[...]).astype(BF16)
    qm = jnp.dot(wqbT_ref[...], qn, preferred_element_type=F32)
    for hh in range(MLA_HEADS):
        blk = qm[128 * hh:128 * (hh + 1)]
        pe = blk[64:96] * cST + blk[96:128] * sST
        qT_ref[0, hh, 0:64, :] = (blk[0:64] * MLA_QS).astype(BF16)
        qT_ref[0, hh, 64:96, :] = (pe * MLA_QS).astype(BF16)
        qT_ref[0, hh, 96:128, :] = zeros32

    ckv = pk[:, 0:128]
    cn = (ckv * _rms_rows(ckv) * gkv_tok_ref[...]).astype(BF16)
    kn = jnp.dot(cn, wkvk_ref[...], preferred_element_type=F32)
    pe_tok = pk[:, 128:256] * cS[:, 0:128] + pk[:, 256:384] * sS[:, 0:128]
    for hh in range(MLA_HEADS):
        k_ref[0, :, 128 * hh:128 * (hh + 1)] = (kn[:, 128 * hh:128 * (hh + 1)] + pe_tok).astype(BF16)

    ckvT = pT[256:384]
    cnT = (ckvT * _rms_cols(ckvT) * gkvT_ref[...]).astype(BF16)
    vmT = jnp.dot(wkvvT_ref[...], cnT, preferred_element_type=F32)
    for hh in range(MLA_HEADS):
        vT_ref[0, hh, 0:64, :] = vmT[64 * hh:64 * (hh + 1)].astype(BF16)
        vT_ref[0, hh, 64:V_ROWS, :] = ones_rows

    qd = pT[384:640].reshape(8, 32, tm)
    qdr = pT[640:896].reshape(8, 32, tm)
    qd = (qd * cST[None] + qdr * sST[None]) * DIFF_QS
    for j in range(8):
        for rb in range(4):
            val = qd[j].astype(BF16) if rb == j % 4 else zeros32
            qT_ref[0, 4 + j, 32 * rb:32 * (rb + 1), :] = val
    kd = pk[:, 384:640] * cS + pk[:, 640:896] * sS
    k_ref[0, :, 512:768] = kd.astype(BF16)
    vdT = pT[1920:2176]
    for hh in range(DIFF_HEADS):
        vT_ref[0, 4 + hh, 0:64, :] = vdT[64 * hh:64 * (hh + 1)].astype(BF16)
        vT_ref[0, 4 + hh, 64:V_ROWS, :] = ones_rows

    gq = pT[896:1408].reshape(8, 64, tm)
    gqr = pT[1408:1920].reshape(8, 64, tm)
    rq = lax.rsqrt(jnp.mean(gq * gq, axis=1, keepdims=True) + EPS)
    qg = ((gq * rq * ggq_ref[...][None]) * cLT[None]
          + (gqr * rq * ggqrot_ref[...][None]) * sLT[None]) * GQA_QS
    for j in range(8):
        grp = j // 4
        for rb in range(2):
            val = qg[j].astype(BF16) if rb == grp else zeros64
            qT_ref[0, 12 + j, 64 * rb:64 * (rb + 1), :] = val
    gk = pk[:, 896:1024]
    gkr = pk[:, 1024:1152]
    sq = gk * gk
    lane = lax.broadcasted_iota(jnp.int32, sq.shape, 1)
    lo = lane < GQA_DIM
    s0 = jnp.sum(jnp.where(lo, sq, 0.0), axis=-1, keepdims=True)
    s1 = jnp.sum(jnp.where(lo, 0.0, sq), axis=-1, keepdims=True)
    rk = jnp.where(lo, lax.rsqrt(s0 / GQA_DIM + EPS), lax.rsqrt(s1 / GQA_DIM + EPS))
    kg = (gk * rk * ggk_ref[...]) * cL + (gkr * rk * ggkrot_ref[...]) * sL
    k_ref[0, :, 768:896] = kg.astype(BF16)
    gvT = pT[2176:2304]
    for hh in range(GQA_KV_HEADS):
        vT_ref[0, 8 + hh, 0:64, :] = gvT[64 * hh:64 * (hh + 1)].astype(BF16)
        vT_ref[0, 8 + hh, 64:V_ROWS, :] = ones_rows


def _token_specs(x_parts):
    n_lat = SEQ // TM
    if len(x_parts) == 1:
        return [pl.BlockSpec((1, TM, D_MODEL), lambda b, t: (b, t, 0))]
    return [pl.BlockSpec((1, TM, D_MODEL), lambda b, t: (b, jnp.minimum(t, n_lat - 1), 0)),
            pl.BlockSpec((1, CTX_LEN, D_MODEL), lambda b, t: (b, 0, 0))]


def _mod_spec(l):
    n_lat = SEQ // TM
    return pl.BlockSpec((1, 1, 6 * D_MODEL),
                        lambda b, t: (8 * l + jnp.where(t == n_lat, 2, b), 0, 0))


def _layer_spec(l, shape, **kwargs):
    return pl.BlockSpec((None,) + shape, lambda b, t, *_: (l,) + (0,) * len(shape), **kwargs)


def _proj(x_parts, mod3, l, lw, tabs):
    bsz = x_parts[0].shape[0]
    t_all = T_ALL
    nt = t_all // TM
    const = functools.partial(_layer_spec, l)
    in_specs = _token_specs(x_parts) + [
        _mod_spec(l),
        const((1, D_MODEL)),
        const((D_MODEL, WK_COLS)),
        const((WQ_ROWS, D_MODEL)),
        const((512, 256)),
        const((128, 512)),
        const((256, 128)),
        const((256, TM)),
        const((1, 128)),
        const((128, TM)),
        const((64, TM)),
        const((64, TM)),
        const((1, 128)),
        const((1, 128)),
        pl.BlockSpec((TM, 256), lambda b, t: (t, 0)),
        pl.BlockSpec((TM, 256), lambda b, t: (t, 0)),
        pl.BlockSpec((TM, 128), lambda b, t: (t, 0)),
        pl.BlockSpec((TM, 128), lambda b, t: (t, 0)),
        pl.BlockSpec((32, TM), lambda b, t: (0, t)),
        pl.BlockSpec((32, TM), lambda b, t: (0, t)),
        pl.BlockSpec((64, TM), lambda b, t: (0, t)),
        pl.BlockSpec((64, TM), lambda b, t: (0, t)),
    ]
    out_specs = [
        pl.BlockSpec((1, N_SCORE_HEADS, 128, TM), lambda b, t: (b, 0, 0, t)),
        pl.BlockSpec((1, TM, N_KEY_SLABS * 128), lambda b, t: (b, t, 0)),
        pl.BlockSpec((1, N_VALUE_HEADS, V_ROWS, TM), lambda b, t: (b, 0, 0, t)),
    ]
    out_shape = [
        jax.ShapeDtypeStruct((bsz, N_SCORE_HEADS, 128, t_all), BF16),
        jax.ShapeDtypeStruct((bsz, t_all, N_KEY_SLABS * 128), BF16),
        jax.ShapeDtypeStruct((bsz, N_VALUE_HEADS, V_ROWS, t_all), BF16),
    ]
    return pl.pallas_call(
        functools.partial(_proj_kernel, n_x_parts=len(x_parts)),
        grid=(bsz, nt),
        in_specs=in_specs,
        out_specs=out_specs,
        out_shape=out_shape,
        compiler_params=pltpu.CompilerParams(
            dimension_semantics=("parallel", "parallel"),
            vmem_limit_bytes=VMEM_LIMIT_BYTES),
        name="proj",
    )(*x_parts, mod3, lw["g_attn_pre"], lw["wk"], lw["wqT"], lw["wqbT"], lw["wkvk"], lw["wkvvT"],
      lw["gqmla_b"], lw["gkv_tok"], lw["gkvT_b"], lw["ggq_b"], lw["ggqrot_b"], lw["ggk128"],
      lw["ggkrot128"],
      tabs["cS"], tabs["sS"], tabs["cL"], tabs["sL"], tabs["cST"], tabs["sST"], tabs["cLT"],
      tabs["sLT"])


def _attn_kernel(qT_ref, k_ref, vT_ref, lam_ref, gsub_ref, yT_ref, acc_ref, m_ref, kept_ref, *,
                 n_k, lam_init):
    ki = pl.program_id(2)
    tk = k_ref.shape[1]
    n_chunks = tk // KEY_CHUNK

    def scores(h, c):
        slab = KEY_SLAB_OF_HEAD[h]
        ks = k_ref[0, KEY_CHUNK * c:KEY_CHUNK * (c + 1), 128 * slab:128 * (slab + 1)]
        return jnp.dot(ks, qT_ref[0, h], preferred_element_type=F32)

    def weighted_values(h, c, pT):
        v = vT_ref[0, VALUE_HEAD_OF_HEAD[h], :, KEY_CHUNK * c:KEY_CHUNK * (c + 1)]
        return jnp.dot(v, pT, preferred_element_type=F32)

    def column_max(m, s_chunks):
        for s in s_chunks:
            m = jnp.maximum(m, jnp.max(s, axis=0, keepdims=True))
        return m

    def sweep_heads(lookahead, head_step):
        pending = [[scores(h, c) for c in range(n_chunks)] for h in range(lookahead)]
        for h in range(N_SCORE_HEADS):
            s_chunks = pending.pop(0)
            issued = []

            def issue_ahead(c, h=h, issued=issued):
                if h + lookahead < N_SCORE_HEADS:
                    issued.append(scores(h + lookahead, c))

            head_step(h, s_chunks, issue_ahead)
            if issued:
                pending.append(issued)

    @pl.when(ki == 0)
    def _first_tile():
        def head_step(h, s_chunks, issue_ahead):
            m_new = column_max(jnp.full((1, s_chunks[0].shape[1]), -jnp.inf, F32), s_chunks)
            oT = None
            for c, s in enumerate(s_chunks):
                issue_ahead(c)
                o_c = weighted_values(h, c, jnp.exp2(s - m_new).astype(BF16))
                oT = o_c if oT is None else oT + o_c
            acc_ref[h] = oT
            m_ref[h] = m_new

        sweep_heads(2, head_step)

    if n_k > 1:
        @pl.when(ki > 0)
        def _later_tiles():
            excess = []

            def head_step(h, s_chunks, issue_ahead):
                m_used = m_ref[h]
                m_tile = m_used
                oT = None
                for c, s in enumerate(s_chunks):
                    m_tile = jnp.maximum(m_tile, jnp.max(s, axis=0, keepdims=True))
                    o_c = weighted_values(h, c, jnp.exp2(s - m_used).astype(BF16))
                    issue_ahead(c)
                    oT = o_c if oT is None else oT + o_c
                acc_old = acc_ref[h]
                over = m_tile - m_used
                keep_old = over > STALE_MAX_SLACK
                acc_ref[h] = jnp.where(keep_old, acc_old, (acc_old + oT) * jnp.exp2(-over))
                m_ref[h] = jnp.where(keep_old, m_used, m_tile)
                kept_ref[h] = jnp.where(keep_old, 1.0, 0.0)
                excess.append(over)

            sweep_heads(1, head_step)
            worst = excess[0]
            for e in excess[1:]:
                worst = jnp.maximum(worst, e)

            @pl.when(jnp.max(worst) > STALE_MAX_SLACK)
            def _redo_kept_columns():
                def redo_head(h, carry):
                    slab = jnp.where(h < 4, h, jnp.where(h < 12, 4 + ((h - 4) >> 2), 6))
                    vh = jnp.where(h < 4, h, jnp.where(h < 12, 4 + ((h - 4) >> 1),
                                                       8 + ((h - 12) >> 2)))
                    ks = k_ref[0, :, pl.ds(pl.multiple_of(slab * 128, 128), 128)]
                    sT = jnp.dot(ks, qT_ref[0, h], preferred_element_type=F32)
                    m_old = m_ref[h]
                    m_new = jnp.maximum(m_old, jnp.max(sT, axis=0, keepdims=True))
                    kept = kept_ref[h] > 0.5
                    oT = jnp.dot(vT_ref[0, vh], jnp.exp2(sT - m_new).astype(BF16),
                                 preferred_element_type=F32)
                    acc_old = acc_ref[h]
                    acc_ref[h] = jnp.where(kept, acc_old * jnp.exp2(m_old - m_new) + oT, acc_old)
                    m_ref[h] = jnp.where(kept, m_new, m_old)
                    return carry

                lax.fori_loop(0, N_SCORE_HEADS, redo_head, 0)

    @pl.when(ki == n_k - 1)
    def _finalize():
        def head_out(h):
            a = acc_ref[h]
            return a[0:HEAD_V] / a[HEAD_V:HEAD_V + 1]
        for hh in range(MLA_HEADS):
            yT_ref[0, 64 * hh:64 * (hh + 1), :] = head_out(hh).astype(BF16)
        lp = lam_ref[...]
        l1 = jnp.sum(lp[0:1] * lp[1:2], axis=-1, keepdims=True)
        l2 = jnp.sum(lp[2:3] * lp[3:4], axis=-1, keepdims=True)
        lam = jnp.exp(l1) - jnp.exp(l2) + lam_init
        gsub = gsub_ref[...]
        for hh in range(DIFF_HEADS):
            d = head_out(4 + 2 * hh) - lam * head_out(5 + 2 * hh)
            y = (d * _rms_cols(d) * gsub) * (1.0 - lam_init)
            yT_ref[0, 256 + 64 * hh:256 + 64 * (hh + 1), :] = y.astype(BF16)
        for j in range(GQA_HEADS):
            yT_ref[0, 512 + 64 * j:512 + 64 * (j + 1), :] = head_out(12 + j).astype(BF16)


def _attn(qT, kslab, vT, lam_vecs, gsub_b, l, lam_init, *, tq, tk, q_blocks, q_off, k_blocks,
          k_off, name):
    bsz = qT.shape[0]
    return pl.pallas_call(
        functools.partial(_attn_kernel, n_k=k_blocks, lam_init=lam_init),
        grid=(bsz, q_blocks, k_blocks),
        in_specs=[
            pl.BlockSpec((1, N_SCORE_HEADS, 128, tq), lambda b, qi, ki: (b, 0, 0, qi + q_off)),
            pl.BlockSpec((1, tk, N_KEY_SLABS * 128), lambda b, qi, ki: (b, ki + k_off, 0)),
            pl.BlockSpec((1, N_VALUE_HEADS, V_ROWS, tk), lambda b, qi, ki: (b, 0, 0, ki + k_off)),
            _layer_spec(l, (4, DIFF_QK)),
            _layer_spec(l, (HEAD_V, tq)),
        ],
        out_specs=pl.BlockSpec((1, MIX_WIDTH, tq), lambda b, qi, ki: (b, 0, qi)),
        out_shape=jax.ShapeDtypeStruct((bsz, MIX_WIDTH, q_blocks * tq), BF16),
        scratch_shapes=[
            pltpu.VMEM((N_SCORE_HEADS, V_ROWS, tq), F32),
            pltpu.VMEM((N_SCORE_HEADS, 1, tq), F32),
            pltpu.VMEM((N_SCORE_HEADS, 1, tq), F32),
        ],
        compiler_params=pltpu.CompilerParams(
            dimension_semantics=("parallel", "parallel", "arbitrary"),
            vmem_limit_bytes=VMEM_LIMIT_BYTES),
        name=name,
    )(qT, kslab, vT, lam_vecs, gsub_b)


def _out_ffn_kernel(*refs, n_lat_tiles, has_ctx, n_x_parts):
    if has_ctx:
        yT_ref, yTc_ref = refs[:2]
        refs = refs[2:]
        yT = jnp.where(pl.program_id(1) == n_lat_tiles, yTc_ref[0], yT_ref[0])
    else:
        yT = refs[0][0]
        refs = refs[1:]
    x = _token_tile(refs, n_x_parts)
    mod_ref, gpost_ref, gfpre_ref, gfpost_ref, wout_ref, wg_ref, wu_ref, wd_ref, o_ref = \
        refs[n_x_parts:]
    yp = lax.dot_general(yT, wout_ref[...], (((0,), (0,)), ((), ())),
                         preferred_element_type=F32)
    mod = mod_ref[0]
    gt_a = mod[:, 2 * D_MODEL:3 * D_MODEL]
    sh_f = mod[:, 3 * D_MODEL:4 * D_MODEL]
    sc_f = mod[:, 4 * D_MODEL:5 * D_MODEL]
    gt_f = mod[:, 5 * D_MODEL:6 * D_MODEL]
    x1 = x + gt_a * (yp * _rms_rows(yp) * gpost_ref[...])
    hf = (x1 * _rms_rows(x1) * gfpre_ref[...]) * (1.0 + sc_f) + sh_f
    hb = hf.astype(BF16)
    g = jnp.dot(hb, wg_ref[...], preferred_element_type=F32)
    u = jnp.dot(hb, wu_ref[...], preferred_element_type=F32)
    a = (g * jax.nn.sigmoid(g)) * u
    f = jnp.dot(a.astype(BF16), wd_ref[...], preferred_element_type=F32)
    o_ref[0] = x1 + gt_f * (f * _rms_rows(f) * gfpost_ref[...])


def _out_ffn(yT, yT_ctx, x_parts, mod3, l, lw):
    bsz = x_parts[0].shape[0]
    n_lat = SEQ // TM
    has_ctx = yT_ctx is not None
    n_tiles = n_lat + 1 if has_ctx else n_lat
    const = functools.partial(_layer_spec, l, pipeline_mode=pl.Buffered(1))
    y_specs = [pl.BlockSpec((1, MIX_WIDTH, TM), lambda b, t: (b, 0, jnp.minimum(t, n_lat - 1)))]
    y_args = [yT]
    if has_ctx:
        y_specs.append(pl.BlockSpec((1, MIX_WIDTH, TM), lambda b, t: (b, 0, 0)))
        y_args.append(yT_ctx)
    return pl.pallas_call(
        functools.partial(_out_ffn_kernel, n_lat_tiles=n_lat, has_ctx=has_ctx,
                          n_x_parts=len(x_parts)),
        grid=(bsz, n_tiles),
        in_specs=y_specs + _token_specs(x_parts) + [
            _mod_spec(l),
            const((1, D_MODEL)),
            const((1, D_MODEL)),
            const((1, D_MODEL)),
            const((MIX_WIDTH, D_MODEL)),
            const((D_MODEL, FFN_HIDDEN)),
            const((D_MODEL, FFN_HIDDEN)),
            const((FFN_HIDDEN, D_MODEL)),
        ],
        out_specs=pl.BlockSpec((1, TM, D_MODEL), lambda b, t: (b, t, 0)),
        out_shape=jax.ShapeDtypeStruct((bsz, n_tiles * TM, D_MODEL), F32),
        compiler_params=pltpu.CompilerParams(
            dimension_semantics=("parallel", "parallel"),
            vmem_limit_bytes=VMEM_LIMIT_BYTES),
        name="out_ffn",
    )(*y_args, *x_parts, mod3, lw["g_attn_post"], lw["g_ffn_pre"], lw["g_ffn_post"],
      lw["w_out"], lw["w_gate"], lw["w_up"], lw["w_down"])


def _rot_cols(w, half):
    n = w.shape[-1]
    wg = w.reshape(w.shape[:-1] + (n // (2 * half), 2, half))
    return jnp.concatenate([-wg[..., 1:2, :], wg[..., 0:1, :]], axis=-2).reshape(w.shape)


def _swap_halves(g, half):
    return jnp.concatenate([g[..., half:], g[..., :half]], axis=-1)


def _rope_tables():
    t = jnp.arange(SEQ, dtype=jnp.int32)
    row = (t // GRID_W).astype(F32)
    col = (t % GRID_W).astype(F32)

    def tables(rot_dim):
        quarter = rot_dim // 4
        inv = ROPE_THETA ** (-jnp.arange(quarter, dtype=F32) / quarter)
        ang = jnp.concatenate([row[:, None] * inv, col[:, None] * inv], axis=-1)
        cos = jnp.concatenate([jnp.cos(ang), jnp.ones((CTX_LEN, rot_dim // 2), F32)], axis=0)
        sin = jnp.concatenate([jnp.sin(ang), jnp.zeros((CTX_LEN, rot_dim // 2), F32)], axis=0)
        return jnp.tile(cos, (1, 2)), jnp.tile(sin, (1, 2))

    c_s, s_s = tables(MLA_ROPE)
    c_l, s_l = tables(GQA_DIM)
    return {
        "cS": jnp.tile(c_s, (1, 8)), "sS": jnp.tile(s_s, (1, 8)),
        "cL": jnp.tile(c_l, (1, 2)), "sL": jnp.tile(s_l, (1, 2)),
        "cST": c_s.T, "sST": s_s.T,
        "cLT": c_l.T, "sLT": s_l.T,
    }


def _stacked_weights(w_in, g_attn_pre, g_attn_post, g_mla_q, w_mla_qb, g_mla_kv, w_mla_kvb,
                     g_diff_sub, g_gqa_q, g_gqa_k, w_out, g_ffn_pre, g_ffn_post, w_ffn_gate,
                     w_ffn_up, w_ffn_down, lambdas):
    depth = w_in.shape[0]
    q_a = w_in[..., 0:256]
    c_kv = w_in[..., 256:384]
    k_pe = w_in[..., 384:416]
    dq = w_in[..., 416:672]
    dk = w_in[..., 672:928]
    dv = w_in[..., 928:1184]
    gq = w_in[..., 1184:1696]
    gk = w_in[..., 1696:1824]
    gv = w_in[..., 1824:1952]
    zeros = lambda n: jnp.zeros((depth, D_MODEL, n), F32)
    pe_slab = jnp.concatenate([zeros(64), k_pe, zeros(32)], axis=-1)
    pe_rot_slab = jnp.concatenate([zeros(64), _rot_cols(k_pe, 16), zeros(32)], axis=-1)
    wk = jnp.concatenate([c_kv, pe_slab, pe_rot_slab, dk, _rot_cols(dk, 16), gk, _rot_cols(gk, 32)],
                         axis=-1)
    wq = jnp.concatenate([q_a, c_kv, dq, _rot_cols(dq, 16), gq, _rot_cols(gq, 32), dv, gv], axis=-1)
    qb = w_mla_qb.reshape(depth, MLA_Q_RANK, MLA_HEADS, MLA_NOPE + MLA_ROPE)
    wqb = jnp.concatenate([qb, _rot_cols(qb[..., MLA_NOPE:], 16)], axis=-1)
    wqb = wqb.reshape(depth, MLA_Q_RANK, 512)
    kvb = w_mla_kvb.reshape(depth, MLA_KV_RANK, MLA_HEADS, MLA_NOPE + MLA_V)
    wkvk = jnp.concatenate([kvb[..., :MLA_NOPE],
                            jnp.zeros((depth, MLA_KV_RANK, MLA_HEADS, 64), F32)], axis=-1)
    wkvk = wkvk.reshape(depth, MLA_KV_RANK, 512)
    wkvv = kvb[..., MLA_NOPE:].reshape(depth, MLA_KV_RANK, MLA_HEADS * MLA_V)
    transposed = lambda w: jnp.swapaxes(w, 1, 2).astype(BF16)
    row = lambda g: g[:, None, :]
    column = lambda g, n: jnp.broadcast_to(g[:, :, None], g.shape + (n,))
    return {
        "g_attn_pre": row(g_attn_pre), "g_attn_post": row(g_attn_post),
        "g_ffn_pre": row(g_ffn_pre), "g_ffn_post": row(g_ffn_post),
        "wk": wk.astype(BF16), "wqT": transposed(wq), "wqbT": transposed(wqb),
        "wkvk": wkvk.astype(BF16), "wkvvT": transposed(wkvv),
        "gqmla_b": column(g_mla_q, TM), "gkv_tok": row(g_mla_kv), "gkvT_b": column(g_mla_kv, TM),
        "ggq_b": column(g_gqa_q, TM), "ggqrot_b": column(_swap_halves(g_gqa_q, 32), TM),
        "ggk128": row(jnp.tile(g_gqa_k, (1, 2))),
        "ggkrot128": row(jnp.tile(_swap_halves(g_gqa_k, 32), (1, 2))),
        "gsub_b": column(g_diff_sub, TQ),
        "lam_vecs": jnp.stack(lambdas, axis=1),
        "w_out": w_out.astype(BF16), "w_gate": w_ffn_gate.astype(BF16),
        "w_up": w_ffn_up.astype(BF16), "w_down": w_ffn_down.astype(BF16),
    }


def kernel(x, c, ctx, c_ctx, w_ada, b_ada, g_attn_pre, g_attn_post, w_in, g_mla_q, w_mla_qb, g_mla_kv, w_mla_kvb, lambda_q1, lambda_k1, lambda_q2, lambda_k2, g_diff_sub, g_gqa_q, g_gqa_k, w_out, g_ffn_pre, g_ffn_post, w_ffn_gate, w_ffn_up, w_ffn_down):
    bsz = x.shape[0]
    depth = w_ada.shape[0]
    assert x.shape == (bsz, SEQ, D_MODEL) and ctx.shape == (bsz, CTX_LEN, D_MODEL) and bsz == 2
    tabs = _rope_tables()
    cvec = jnp.concatenate([c, c_ctx[None], jnp.zeros((8 - bsz - 1, D_MODEL), F32)], axis=0)
    mod3 = _adaln(cvec, w_ada, b_ada).reshape(depth * 8, 1, 6 * D_MODEL)
    lw = _stacked_weights(w_in, g_attn_pre, g_attn_post, g_mla_q, w_mla_qb, g_mla_kv, w_mla_kvb,
                          g_diff_sub, g_gqa_q, g_gqa_k, w_out, g_ffn_pre, g_ffn_post, w_ffn_gate,
                          w_ffn_up, w_ffn_down, (lambda_q1, lambda_k1, lambda_q2, lambda_k2))
    x_parts = (x, ctx)
    for l in range(depth):
        last = l == depth - 1
        lam_init = 0.8 - 0.6 * math.exp(-0.3 * l)
        qT, kslab, vT = _proj(x_parts, mod3, l, lw, tabs)
        yT = _attn(qT, kslab, vT, lw["lam_vecs"], lw["gsub_b"], l, lam_init, tq=TQ, tk=TK,
                   q_blocks=SEQ // TQ, q_off=0, k_blocks=T_ALL // TK, k_off=0, name="attn")
        yT_ctx = None
        if not last:
            yT_ctx = _attn(qT, kslab, vT, lw["lam_vecs"], lw["gsub_b"], l, lam_init,
                           tq=CTX_LEN, tk=CTX_LEN, q_blocks=1, q_off=SEQ // CTX_LEN, k_blocks=1,
                           k_off=SEQ // CTX_LEN, name="attn_ctx")
        x_parts = (_out_ffn(yT, yT_ctx, x_parts, mod3, l, lw),)
    return x_parts[0]
```

```python
import functools
import math

import jax
import jax.numpy as jnp
from jax import lax
from jax.experimental import pallas as pl
from jax.experimental.pallas import tpu as pltpu

F32 = jnp.float32
BF16 = jnp.bfloat16

D_MODEL = 1024
SEQ = 8192
GRID_W = 64
CTX_LEN = 256
T_ALL = SEQ + CTX_LEN
ROPE_THETA = 10000.0
EPS = 1e-6

MLA_HEADS = 4
MLA_Q_RANK = 256
MLA_KV_RANK = 128
MLA_NOPE = 64
MLA_ROPE = 32
MLA_V = 64
DIFF_HEADS = 4
DIFF_QK = 32
DIFF_V = 64
GQA_HEADS = 8
GQA_KV_HEADS = 2
GQA_DIM = 64
HEAD_V = 64
MIX_WIDTH = 1024
FFN_HIDDEN = 2816

LOG2E = 1.4426950408889634
MLA_QS = LOG2E / math.sqrt(MLA_NOPE + MLA_ROPE)
DIFF_QS = LOG2E / math.sqrt(DIFF_QK)
GQA_QS = LOG2E / math.sqrt(GQA_DIM)

BF16_SUBLANES = 16
MIB = 1024 * 1024
VMEM_LIMIT_ADALN = 16 * MIB
VMEM_LIMIT_PROJ = 32 * MIB
VMEM_LIMIT_ATTN = 32 * MIB
VMEM_LIMIT_OUT_FFN = 48 * MIB

N_SCORE_HEADS = 20
N_KEY_SLABS = 7
N_VALUE_HEADS = 10
V_ROWS = HEAD_V + BF16_SUBLANES
KEY_SLAB_OF_HEAD = (0, 1, 2, 3, 4, 4, 4, 4, 5, 5, 5, 5, 6, 6, 6, 6, 6, 6, 6, 6)
VALUE_HEAD_OF_HEAD = (0, 1, 2, 3, 4, 4, 5, 5, 6, 6, 7, 7, 8, 8, 8, 8, 9, 9, 9, 9)

TM = 256
TQ = 512
TK = 768
KEY_CHUNK = 256
STALE_MAX_SLACK = 64.0

WK_COLS = 1152
WQ_ROWS = 2304


def _rms_rows(x, eps=EPS):
    return lax.rsqrt(jnp.mean(x * x, axis=-1, keepdims=True) + eps)


def _rms_cols(x, eps=EPS):
    return lax.rsqrt(jnp.mean(x * x, axis=0, keepdims=True) + eps)


def _adaln_kernel(c_ref, w_ref, b_ref, o_ref):
    c = c_ref[...]
    sc = c * jax.nn.sigmoid(c)
    o_ref[0] = jnp.dot(sc.astype(BF16), w_ref[0].astype(BF16),
                       preferred_element_type=F32) + b_ref[0]


def _adaln(cvec, w_ada, b_ada):
    depth = w_ada.shape[0]
    tn = 1024
    n_out = w_ada.shape[2]
    return pl.pallas_call(
        _adaln_kernel,
        grid=(depth, n_out // tn),
        in_specs=[
            pl.BlockSpec((8, D_MODEL), lambda l, j: (0, 0)),
            pl.BlockSpec((1, D_MODEL, tn), lambda l, j: (l, 0, j)),
            pl.BlockSpec((1, 1, tn), lambda l, j: (l, 0, j)),
        ],
        out_specs=pl.BlockSpec((1, 8, tn), lambda l, j: (l, 0, j)),
        out_shape=jax.ShapeDtypeStruct((depth, 8, n_out), F32),
        compiler_params=pltpu.CompilerParams(
            dimension_semantics=("arbitrary", "arbitrary"),
            vmem_limit_bytes=VMEM_LIMIT_ADALN),
        name="adaln",
    )(cvec, w_ada, b_ada.reshape(depth, 1, n_out))


def _token_tile(refs, n_parts):
    if n_parts == 1:
        return refs[0][0]
    return jnp.where(pl.program_id(1) == SEQ // TM, refs[1][0], refs[0][0])


def _proj_kernel(*refs, n_x_parts):
    x = _token_tile(refs, n_x_parts)
    (mod_ref, gpre_ref, wk_ref, wqT_ref, wqbT_ref, wkvk_ref, wkvvT_ref,
     gqmla_ref, gkv_tok_ref, gkvT_ref, ggq_ref, ggqrot_ref, ggk_ref, ggkrot_ref,
     cS_ref, sS_ref, cL_ref, sL_ref, cST_ref, sST_ref, cLT_ref, sLT_ref,
     qT_ref, k_ref, vT_ref) = refs[n_x_parts:]
    tm = x.shape[0]
    mod = mod_ref[0]
    sh = mod[:, 0:D_MODEL]
    sc = mod[:, D_MODEL:2 * D_MODEL]
    h = (x * _rms_rows(x) * gpre_ref[...]) * (1.0 + sc) + sh
    hb = h.astype(BF16)
    pk = jnp.dot(hb, wk_ref[...], preferred_element_type=F32)
    pT = lax.dot_general(wqT_ref[...], hb, (((1,), (1,)), ((), ())),
                         preferred_element_type=F32)

    cS = cS_ref[...]
    sS = sS_ref[...]
    cL = cL_ref[...]
    sL = sL_ref[...]
    cST = cST_ref[...]
    sST = sST_ref[...]
    cLT = cLT_ref[...]
    sLT = sLT_ref[...]
    ones_rows = jnp.ones((BF16_SUBLANES, tm), BF16)
    zeros32 = jnp.zeros((32, tm), BF16)
    zeros64 = jnp.zeros((64, tm), BF16)

    qaT = pT[0:256]
    qn = (qaT * _rms_cols(qaT) * gqmla_ref[...]).astype(BF16)
    qm = jnp.dot(wqbT_ref[...], qn, preferred_element_type=F32)
    for hh in range(MLA_HEADS):
        blk = qm[128 * hh:128 * (hh + 1)]
        pe = blk[64:96] * cST + blk[96:128] * sST
        qT_ref[0, hh, 0:64, :] = (blk[0:64] * MLA_QS).astype(BF16)
        qT_ref[0, hh, 64:96, :] = (pe * MLA_QS).astype(BF16)
        qT_ref[0, hh, 96:128, :] = zeros32

    ckv = pk[:, 0:128]
    cn = (ckv * _rms_rows(ckv) * gkv_tok_ref[...]).astype(BF16)
    kn = jnp.dot(cn, wkvk_ref[...], preferred_element_type=F32)
    pe_tok = pk[:, 128:256] * cS[:, 0:128] + pk[:, 256:384] * sS[:, 0:128]
    for hh in range(MLA_HEADS):
        k_ref[0, :, 128 * hh:128 * (hh + 1)] = (kn[:, 128 * hh:128 * (hh + 1)] + pe_tok).astype(BF16)

    ckvT = pT[256:384]
    cnT = (ckvT * _rms_cols(ckvT) * gkvT_ref[...]).astype(BF16)
    vmT = jnp.dot(wkvvT_ref[...], cnT, preferred_element_type=F32)
    for hh in range(MLA_HEADS):
        vT_ref[0, hh, 0:64, :] = vmT[64 * hh:64 * (hh + 1)].astype(BF16)
        vT_ref[0, hh, 64:V_ROWS, :] = ones_rows

    qd = pT[384:640].reshape(8, 32, tm)
    qdr = pT[640:896].reshape(8, 32, tm)
    qd = (qd * cST[None] + qdr * sST[None]) * DIFF_QS
    for j in range(8):
        for rb in range(4):
            val = qd[j].astype(BF16) if rb == j % 4 else zeros32
            qT_ref[0, 4 + j, 32 * rb:32 * (rb + 1), :] = val
    kd = pk[:, 384:640] * cS + pk[:, 640:896] * sS
    k_ref[0, :, 512:768] = kd.astype(BF16)
    vdT = pT[1920:2176]
    for hh in range(DIFF_HEADS):
        vT_ref[0, 4 + hh, 0:64, :] = vdT[64 * hh:64 * (hh + 1)].astype(BF16)
        vT_ref[0, 4 + hh, 64:V_ROWS, :] = ones_rows

    gq = pT[896:1408].reshape(8, 64, tm)
    gqr = pT[1408:1920].reshape(8, 64, tm)
    rq = lax.rsqrt(jnp.mean(gq * gq, axis=1, keepdims=True) + EPS)
    qg = ((gq * rq * ggq_ref[...][None]) * cLT[None]
          + (gqr * rq * ggqrot_ref[...][None]) * sLT[None]) * GQA_QS
    for j in range(8):
        grp = j // 4
        for rb in range(2):
            val = qg[j].astype(BF16) if rb == grp else zeros64
            qT_ref[0, 12 + j, 64 * rb:64 * (rb + 1), :] = val
    gk = pk[:, 896:1024]
    gkr = pk[:, 1024:1152]
    sq = gk * gk
    lane = lax.broadcasted_iota(jnp.int32, sq.shape, 1)
    lo = lane < GQA_DIM
    s0 = jnp.sum(jnp.where(lo, sq, 0.0), axis=-1, keepdims=True)
    s1 = jnp.sum(jnp.where(lo, 0.0, sq), axis=-1, keepdims=True)
    rk = jnp.where(lo, lax.rsqrt(s0 / GQA_DIM + EPS), lax.rsqrt(s1 / GQA_DIM + EPS))
    kg = (gk * rk * ggk_ref[...]) * cL + (gkr * rk * ggkrot_ref[...]) * sL
    k_ref[0, :, 768:896] = kg.astype(BF16)
    gvT = pT[2176:2304]
    for hh in range(GQA_KV_HEADS):
        vT_ref[0, 8 + hh, 0:64, :] = gvT[64 * hh:64 * (hh + 1)].astype(BF16)
        vT_ref[0, 8 + hh, 64:V_ROWS, :] = ones_rows


def _token_specs(x_parts):
    n_lat = SEQ // TM
    if len(x_parts) == 1:
        return [pl.BlockSpec((1, TM, D_MODEL), lambda b, t: (b, t, 0))]
    return [pl.BlockSpec((1, TM, D_MODEL), lambda b, t: (b, jnp.minimum(t, n_lat - 1), 0)),
            pl.BlockSpec((1, CTX_LEN, D_MODEL), lambda b, t: (b, 0, 0))]


def _mod_spec(l):
    n_lat = SEQ // TM
    return pl.BlockSpec((1, 1, 6 * D_MODEL),
                        lambda b, t: (8 * l + jnp.where(t == n_lat, 2, b), 0, 0))


def _layer_spec(l, shape, **kwargs):
    return pl.BlockSpec((None,) + shape, lambda b, t, *_: (l,) + (0,) * len(shape), **kwargs)


def _proj(x_parts, mod3, l, lw, tabs):
    bsz = x_parts[0].shape[0]
    t_all = T_ALL
    nt = t_all // TM
    const = functools.partial(_layer_spec, l)
    in_specs = _token_specs(x_parts) + [
        _mod_spec(l),
        const((1, D_MODEL)),
        const((D_MODEL, WK_COLS)),
        const((WQ_ROWS, D_MODEL)),
        const((512, 256)),
        const((128, 512)),
        const((256, 128)),
        const((256, TM)),
        const((1, 128)),
        const((128, TM)),
        const((64, TM)),
        const((64, TM)),
        const((1, 128)),
        const((1, 128)),
        pl.BlockSpec((TM, 256), lambda b, t: (t, 0)),
        pl.BlockSpec((TM, 256), lambda b, t: (t, 0)),
        pl.BlockSpec((TM, 128), lambda b, t: (t, 0)),
        pl.BlockSpec((TM, 128), lambda b, t: (t, 0)),
        pl.BlockSpec((32, TM), lambda b, t: (0, t)),
        pl.BlockSpec((32, TM), lambda b, t: (0, t)),
        pl.BlockSpec((64, TM), lambda b, t: (0, t)),
        pl.BlockSpec((64, TM), lambda b, t: (0, t)),
    ]
    out_specs = [
        pl.BlockSpec((1, N_SCORE_HEADS, 128, TM), lambda b, t: (b, 0, 0, t)),
        pl.BlockSpec((1, TM, N_KEY_SLABS * 128), lambda b, t: (b, t, 0)),
        pl.BlockSpec((1, N_VALUE_HEADS, V_ROWS, TM), lambda b, t: (b, 0, 0, t)),
    ]
    out_shape = [
        jax.ShapeDtypeStruct((bsz, N_SCORE_HEADS, 128, t_all), BF16),
        jax.ShapeDtypeStruct((bsz, t_all, N_KEY_SLABS * 128), BF16),
        jax.ShapeDtypeStruct((bsz, N_VALUE_HEADS, V_ROWS, t_all), BF16),
    ]
    return pl.pallas_call(
        functools.partial(_proj_kernel, n_x_parts=len(x_parts)),
        grid=(bsz, nt),
        in_specs=in_specs,
        out_specs=out_specs,
        out_shape=out_shape,
        compiler_params=pltpu.CompilerParams(
            dimension_semantics=("parallel", "parallel"),
            vmem_limit_bytes=VMEM_LIMIT_PROJ),
        name="proj",
    )(*x_parts, mod3, lw["g_attn_pre"], lw["wk"], lw["wqT"], lw["wqbT"], lw["wkvk"], lw["wkvvT"],
      lw["gqmla_b"], lw["gkv_tok"], lw["gkvT_b"], lw["ggq_b"], lw["ggqrot_b"], lw["ggk128"],
      lw["ggkrot128"],
      tabs["cS"], tabs["sS"], tabs["cL"], tabs["sL"], tabs["cST"], tabs["sST"], tabs["cLT"],
      tabs["sLT"])


def _attn_kernel(qT_ref, k_ref, vT_ref, lam_ref, gsub_ref, yT_ref, acc_ref, m_ref, kept_ref, *,
                 n_k, lam_init):
    ki = pl.program_id(2)
    tk = k_ref.shape[1]
    n_chunks = tk // KEY_CHUNK

    def scores(h, c):
        slab = KEY_SLAB_OF_HEAD[h]
        ks = k_ref[0, KEY_CHUNK * c:KEY_CHUNK * (c + 1), 128 * slab:128 * (slab + 1)]
        return jnp.dot(ks, qT_ref[0, h], preferred_element_type=F32)

    def weighted_values(h, c, pT):
        v = vT_ref[0, VALUE_HEAD_OF_HEAD[h], :, KEY_CHUNK * c:KEY_CHUNK * (c + 1)]
        return jnp.dot(v, pT, preferred_element_type=F32)

    def column_max(m, s_chunks):
        for s in s_chunks:
            m = jnp.maximum(m, jnp.max(s, axis=0, keepdims=True))
        return m

    def sweep_heads(lookahead, head_step):
        pending = [[scores(h, c) for c in range(n_chunks)] for h in range(lookahead)]
        for h in range(N_SCORE_HEADS):
            s_chunks = pending.pop(0)
            issued = []

            def issue_ahead(c, h=h, issued=issued):
                if h + lookahead < N_SCORE_HEADS:
                    issued.append(scores(h + lookahead, c))

            head_step(h, s_chunks, issue_ahead)
            if issued:
                pending.append(issued)

    @pl.when(ki == 0)
    def _first_tile():
        def head_step(h, s_chunks, issue_ahead):
            m_new = column_max(jnp.full((1, s_chunks[0].shape[1]), -jnp.inf, F32), s_chunks)
            oT = None
            for c, s in enumerate(s_chunks):
                issue_ahead(c)
                o_c = weighted_values(h, c, jnp.exp2(s - m_new).astype(BF16))
                oT = o_c if oT is None else oT + o_c
            acc_ref[h] = oT
            m_ref[h] = m_new

        sweep_heads(2, head_step)

    if n_k > 1:
        @pl.when(ki > 0)
        def _later_tiles():
            excess = []

            def head_step(h, s_chunks, issue_ahead):
                m_used = m_ref[h]
                m_tile = m_used
                oT = None
                for c, s in enumerate(s_chunks):
                    m_tile = jnp.maximum(m_tile, jnp.max(s, axis=0, keepdims=True))
                    o_c = weighted_values(h, c, jnp.exp2(s - m_used).astype(BF16))
                    issue_ahead(c)
                    oT = o_c if oT is None else oT + o_c
                acc_old = acc_ref[h]
                over = m_tile - m_used
                keep_old = over > STALE_MAX_SLACK
                acc_ref[h] = jnp.where(keep_old, acc_old, (acc_old + oT) * jnp.exp2(-over))
                m_ref[h] = jnp.where(keep_old, m_used, m_tile)
                kept_ref[h] = jnp.where(keep_old, 1.0, 0.0)
                excess.append(over)

            sweep_heads(1, head_step)
            worst = excess[0]
            for e in excess[1:]:
                worst = jnp.maximum(worst, e)

            @pl.when(jnp.max(worst) > STALE_MAX_SLACK)
            def _redo_kept_columns():
                def redo_head(h, carry):
                    slab = jnp.where(h < 4, h, jnp.where(h < 12, 4 + ((h - 4) >> 2), 6))
                    vh = jnp.where(h < 4, h, jnp.where(h < 12, 4 + ((h - 4) >> 1),
                                                       8 + ((h - 12) >> 2)))
                    ks = k_ref[0, :, pl.ds(pl.multiple_of(slab * 128, 128), 128)]
                    sT = jnp.dot(ks, qT_ref[0, h], preferred_element_type=F32)
                    m_old = m_ref[h]
                    m_new = jnp.maximum(m_old, jnp.max(sT, axis=0, keepdims=True))
                    kept = kept_ref[h] > 0.5
                    oT = jnp.dot(vT_ref[0, vh], jnp.exp2(sT - m_new).astype(BF16),
                                 preferred_element_type=F32)
                    acc_old = acc_ref[h]
                    acc_ref[h] = jnp.where(kept, acc_old * jnp.exp2(m_old - m_new) + oT, acc_old)
                    m_ref[h] = jnp.where(kept, m_new, m_old)
                    return carry

                lax.fori_loop(0, N_SCORE_HEADS, redo_head, 0)

    @pl.when(ki == n_k - 1)
    def _finalize():
        def head_out(h):
            a = acc_ref[h]
            return a[0:HEAD_V] / a[HEAD_V:HEAD_V + 1]
        for hh in range(MLA_HEADS):
            yT_ref[0, 64 * hh:64 * (hh + 1), :] = head_out(hh).astype(BF16)
        lp = lam_ref[...]
        l1 = jnp.sum(lp[0:1] * lp[1:2], axis=-1, keepdims=True)
        l2 = jnp.sum(lp[2:3] * lp[3:4], axis=-1, keepdims=True)
        lam = jnp.exp(l1) - jnp.exp(l2) + lam_init
        gsub = gsub_ref[...]
        for hh in range(DIFF_HEADS):
            d = head_out(4 + 2 * hh) - lam * head_out(5 + 2 * hh)
            y = (d * _rms_cols(d) * gsub) * (1.0 - lam_init)
            yT_ref[0, 256 + 64 * hh:256 + 64 * (hh + 1), :] = y.astype(BF16)
        for j in range(GQA_HEADS):
            yT_ref[0, 512 + 64 * j:512 + 64 * (j + 1), :] = head_out(12 + j).astype(BF16)


def _attn(qT, kslab, vT, lam_vecs, gsub_b, l, lam_init, *, tq, tk, q_blocks, q_off, k_blocks,
          k_off, name):
    bsz = qT.shape[0]
    return pl.pallas_call(
        functools.partial(_attn_kernel, n_k=k_blocks, lam_init=lam_init),
        grid=(bsz, q_blocks, k_blocks),
        in_specs=[
            pl.BlockSpec((1, N_SCORE_HEADS, 128, tq), lambda b, qi, ki: (b, 0, 0, qi + q_off)),
            pl.BlockSpec((1, tk, N_KEY_SLABS * 128), lambda b, qi, ki: (b, ki + k_off, 0)),
            pl.BlockSpec((1, N_VALUE_HEADS, V_ROWS, tk), lambda b, qi, ki: (b, 0, 0, ki + k_off)),
            _layer_spec(l, (4, DIFF_QK)),
            _layer_spec(l, (HEAD_V, tq)),
        ],
        out_specs=pl.BlockSpec((1, MIX_WIDTH, tq), lambda b, qi, ki: (b, 0, qi)),
        out_shape=jax.ShapeDtypeStruct((bsz, MIX_WIDTH, q_blocks * tq), BF16),
        scratch_shapes=[
            pltpu.VMEM((N_SCORE_HEADS, V_ROWS, tq), F32),
            pltpu.VMEM((N_SCORE_HEADS, 1, tq), F32),
            pltpu.VMEM((N_SCORE_HEADS, 1, tq), F32),
        ],
        compiler_params=pltpu.CompilerParams(
            dimension_semantics=("parallel", "parallel", "arbitrary"),
            vmem_limit_bytes=VMEM_LIMIT_ATTN),
        name=name,
    )(qT, kslab, vT, lam_vecs, gsub_b)


def _out_ffn_kernel(*refs, n_lat_tiles, has_ctx, n_x_parts):
    if has_ctx:
        yT_ref, yTc_ref = refs[:2]
        refs = refs[2:]
        yT = jnp.where(pl.program_id(1) == n_lat_tiles, yTc_ref[0], yT_ref[0])
    else:
        yT = refs[0][0]
        refs = refs[1:]
    x = _token_tile(refs, n_x_parts)
    mod_ref, gpost_ref, gfpre_ref, gfpost_ref, wout_ref, wg_ref, wu_ref, wd_ref, o_ref = \
        refs[n_x_parts:]
    yp = lax.dot_general(yT, wout_ref[...], (((0,), (0,)), ((), ())),
                         preferred_element_type=F32)
    mod = mod_ref[0]
    gt_a = mod[:, 2 * D_MODEL:3 * D_MODEL]
    sh_f = mod[:, 3 * D_MODEL:4 * D_MODEL]
    sc_f = mod[:, 4 * D_MODEL:5 * D_MODEL]
    gt_f = mod[:, 5 * D_MODEL:6 * D_MODEL]
    x1 = x + gt_a * (yp * _rms_rows(yp) * gpost_ref[...])
    hf = (x1 * _rms_rows(x1) * gfpre_ref[...]) * (1.0 + sc_f) + sh_f
    hb = hf.astype(BF16)
    g = jnp.dot(hb, wg_ref[...], preferred_element_type=F32)
    u = jnp.dot(hb, wu_ref[...], preferred_element_type=F32)
    a = (g * jax.nn.sigmoid(g)) * u
    f = jnp.dot(a.astype(BF16), wd_ref[...], preferred_element_type=F32)
    o_ref[0] = x1 + gt_f * (f * _rms_rows(f) * gfpost_ref[...])


def _out_ffn(yT, yT_ctx, x_parts, mod3, l, lw):
    bsz = x_parts[0].shape[0]
    n_lat = SEQ // TM
    has_ctx = yT_ctx is not None
    n_tiles = n_lat + 1 if has_ctx else n_lat
    const = functools.partial(_layer_spec, l, pipeline_mode=pl.Buffered(1))
    y_specs = [pl.BlockSpec((1, MIX_WIDTH, TM), lambda b, t: (b, 0, jnp.minimum(t, n_lat - 1)))]
    y_args = [yT]
    if has_ctx:
        y_specs.append(pl.BlockSpec((1, MIX_WIDTH, TM), lambda b, t: (b, 0, 0)))
        y_args.append(yT_ctx)
    return pl.pallas_call(
        functools.partial(_out_ffn_kernel, n_lat_tiles=n_lat, has_ctx=has_ctx,
                          n_x_parts=len(x_parts)),
        grid=(bsz, n_tiles),
        in_specs=y_specs + _token_specs(x_parts) + [
            _mod_spec(l),
            const((1, D_MODEL)),
            const((1, D_MODEL)),
            const((1, D_MODEL)),
            const((MIX_WIDTH, D_MODEL)),
            const((D_MODEL, FFN_HIDDEN)),
            const((D_MODEL, FFN_HIDDEN)),
            const((FFN_HIDDEN, D_MODEL)),
        ],
        out_specs=pl.BlockSpec((1, TM, D_MODEL), lambda b, t: (b, t, 0)),
        out_shape=jax.ShapeDtypeStruct((bsz, n_tiles * TM, D_MODEL), F32),
        compiler_params=pltpu.CompilerParams(
            dimension_semantics=("parallel", "parallel"),
            vmem_limit_bytes=VMEM_LIMIT_OUT_FFN),
        name="out_ffn",
    )(*y_args, *x_parts, mod3, lw["g_attn_post"], lw["g_ffn_pre"], lw["g_ffn_post"],
      lw["w_out"], lw["w_gate"], lw["w_up"], lw["w_down"])


def _rot_cols(w, half):
    n = w.shape[-1]
    wg = w.reshape(w.shape[:-1] + (n // (2 * half), 2, half))
    return jnp.concatenate([-wg[..., 1:2, :], wg[..., 0:1, :]], axis=-2).reshape(w.shape)


def _swap_halves(g, half):
    return jnp.concatenate([g[..., half:], g[..., :half]], axis=-1)


def _rope_tables():
    t = jnp.arange(SEQ, dtype=jnp.int32)
    row = (t // GRID_W).astype(F32)
    col = (t % GRID_W).astype(F32)

    def tables(rot_dim):
        quarter = rot_dim // 4
        inv = ROPE_THETA ** (-jnp.arange(quarter, dtype=F32) / quarter)
        ang = jnp.concatenate([row[:, None] * inv, col[:, None] * inv], axis=-1)
        cos = jnp.concatenate([jnp.cos(ang), jnp.ones((CTX_LEN, rot_dim // 2), F32)], axis=0)
        sin = jnp.concatenate([jnp.sin(ang), jnp.zeros((CTX_LEN, rot_dim // 2), F32)], axis=0)
        return jnp.tile(cos, (1, 2)), jnp.tile(sin, (1, 2))

    c_s, s_s = tables(MLA_ROPE)
    c_l, s_l = tables(GQA_DIM)
    return {
        "cS": jnp.tile(c_s, (1, 8)), "sS": jnp.tile(s_s, (1, 8)),
        "cL": jnp.tile(c_l, (1, 2)), "sL": jnp.tile(s_l, (1, 2)),
        "cST": c_s.T, "sST": s_s.T,
        "cLT": c_l.T, "sLT": s_l.T,
    }


def _stacked_weights(w_in, g_attn_pre, g_attn_post, g_mla_q, w_mla_qb, g_mla_kv, w_mla_kvb,
                     g_diff_sub, g_gqa_q, g_gqa_k, w_out, g_ffn_pre, g_ffn_post, w_ffn_gate,
                     w_ffn_up, w_ffn_down, lambdas):
    depth = w_in.shape[0]
    q_a = w_in[..., 0:256]
    c_kv = w_in[..., 256:384]
    k_pe = w_in[..., 384:416]
    dq = w_in[..., 416:672]
    dk = w_in[..., 672:928]
    dv = w_in[..., 928:1184]
    gq = w_in[..., 1184:1696]
    gk = w_in[..., 1696:1824]
    gv = w_in[..., 1824:1952]
    zeros = lambda n: jnp.zeros((depth, D_MODEL, n), F32)
    pe_slab = jnp.concatenate([zeros(64), k_pe, zeros(32)], axis=-1)
    pe_rot_slab = jnp.concatenate([zeros(64), _rot_cols(k_pe, 16), zeros(32)], axis=-1)
    wk = jnp.concatenate([c_kv, pe_slab, pe_rot_slab, dk, _rot_cols(dk, 16), gk, _rot_cols(gk, 32)],
                         axis=-1)
    wq = jnp.concatenate([q_a, c_kv, dq, _rot_cols(dq, 16), gq, _rot_cols(gq, 32), dv, gv], axis=-1)
    qb = w_mla_qb.reshape(depth, MLA_Q_RANK, MLA_HEADS, MLA_NOPE + MLA_ROPE)
    wqb = jnp.concatenate([qb, _rot_cols(qb[..., MLA_NOPE:], 16)], axis=-1)
    wqb = wqb.reshape(depth, MLA_Q_RANK, 512)
    kvb = w_mla_kvb.reshape(depth, MLA_KV_RANK, MLA_HEADS, MLA_NOPE + MLA_V)
    wkvk = jnp.concatenate([kvb[..., :MLA_NOPE],
                            jnp.zeros((depth, MLA_KV_RANK, MLA_HEADS, 64), F32)], axis=-1)
    wkvk = wkvk.reshape(depth, MLA_KV_RANK, 512)
    wkvv = kvb[..., MLA_NOPE:].reshape(depth, MLA_KV_RANK, MLA_HEADS * MLA_V)
    transposed = lambda w: jnp.swapaxes(w, 1, 2).astype(BF16)
    row = lambda g: g[:, None, :]
    column = lambda g, n: jnp.broadcast_to(g[:, :, None], g.shape + (n,))
    return {
        "g_attn_pre": row(g_attn_pre), "g_attn_post": row(g_attn_post),
        "g_ffn_pre": row(g_ffn_pre), "g_ffn_post": row(g_ffn_post),
        "wk": wk.astype(BF16), "wqT": transposed(wq), "wqbT": transposed(wqb),
        "wkvk": wkvk.astype(BF16), "wkvvT": transposed(wkvv),
        "gqmla_b": column(g_mla_q, TM), "gkv_tok": row(g_mla_kv), "gkvT_b": column(g_mla_kv, TM),
        "ggq_b": column(g_gqa_q, TM), "ggqrot_b": column(_swap_halves(g_gqa_q, 32), TM),
        "ggk128": row(jnp.tile(g_gqa_k, (1, 2))),
        "ggkrot128": row(jnp.tile(_swap_halves(g_gqa_k, 32), (1, 2))),
        "gsub_b": column(g_diff_sub, TQ),
        "lam_vecs": jnp.stack(lambdas, axis=1),
        "w_out": w_out.astype(BF16), "w_gate": w_ffn_gate.astype(BF16),
        "w_up": w_ffn_up.astype(BF16), "w_down": w_ffn_down.astype(BF16),
    }


def kernel(x, c, ctx, c_ctx, w_ada, b_ada, g_attn_pre, g_attn_post, w_in, g_mla_q, w_mla_qb, g_mla_kv, w_mla_kvb, lambda_q1, lambda_k1, lambda_q2, lambda_k2, g_diff_sub, g_gqa_q, g_gqa_k, w_out, g_ffn_pre, g_ffn_post, w_ffn_gate, w_ffn_up, w_ffn_down):
    bsz = x.shape[0]
    depth = w_ada.shape[0]
    assert x.shape == (bsz, SEQ, D_MODEL) and ctx.shape == (bsz, CTX_LEN, D_MODEL) and bsz == 2
    tabs = _rope_tables()
    cvec = jnp.concatenate([c, c_ctx[None], jnp.zeros((8 - bsz - 1, D_MODEL), F32)], axis=0)
    mod3 = _adaln(cvec, w_ada, b_ada).reshape(depth * 8, 1, 6 * D_MODEL)
    lw = _stacked_weights(w_in, g_attn_pre, g_attn_post, g_mla_q, w_mla_qb, g_mla_kv, w_mla_kvb,
                          g_diff_sub, g_gqa_q, g_gqa_k, w_out, g_ffn_pre, g_ffn_post, w_ffn_gate,
                          w_ffn_up, w_ffn_down, (lambda_q1, lambda_k1, lambda_q2, lambda_k2))
    x_parts = (x, ctx)
    for l in range(depth):
        last = l == depth - 1
        lam_init = 0.8 - 0.6 * math.exp(-0.3 * l)
        qT, kslab, vT = _proj(x_parts, mod3, l, lw, tabs)
        yT = _attn(qT, kslab, vT, lw["lam_vecs"], lw["gsub_b"], l, lam_init, tq=TQ, tk=TK,
                   q_blocks=SEQ // TQ, q_off=0, k_blocks=T_ALL // TK, k_off=0, name="attn")
        yT_ctx = None
        if not last:
            yT_ctx = _attn(qT, kslab, vT, lw["lam_vecs"], lw["gsub_b"], l, lam_init,
                           tq=CTX_LEN, tk=CTX_LEN, q_blocks=1, q_off=SEQ // CTX_LEN, k_blocks=1,
                           k_off=SEQ // CTX_LEN, name="attn_ctx")
        x_parts = (_out_ffn(yT, yT_ctx, x_parts, mod3, l, lw),)
    return x_parts[0]
```

```python
import functools
import math

import jax
import jax.numpy as jnp
from jax import lax
from jax.experimental import pallas as pl
from jax.experimental.pallas import tpu as pltpu

F32 = jnp.float32
BF16 = jnp.bfloat16

D_MODEL = 1024
SEQ = 8192
GRID_W = 64
CTX_LEN = 256
T_ALL = SEQ + CTX_LEN
ROPE_THETA = 10000.0
EPS = 1e-6

MLA_HEADS = 4
MLA_Q_RANK = 256
MLA_KV_RANK = 128
MLA_NOPE = 64
MLA_ROPE = 32
MLA_V = 64
DIFF_HEADS = 4
DIFF_QK = 32
DIFF_V = 64
GQA_HEADS = 8
GQA_KV_HEADS = 2
GQA_DIM = 64
HEAD_V = 64
MIX_WIDTH = 1024
FFN_HIDDEN = 2816

LOG2E = 1.4426950408889634
MLA_QS = LOG2E / math.sqrt(MLA_NOPE + MLA_ROPE)
DIFF_QS = LOG2E / math.sqrt(DIFF_QK)
GQA_QS = LOG2E / math.sqrt(GQA_DIM)

BF16_SUBLANES = 16
MIB = 1024 * 1024
VMEM_LIMIT_ADALN = 16 * MIB
VMEM_LIMIT_PROJ = 32 * MIB
VMEM_LIMIT_ATTN = 32 * MIB
VMEM_LIMIT_OUT_FFN = 48 * MIB

N_SCORE_HEADS = 20
N_KEY_SLABS = 7
N_VALUE_HEADS = 10
V_ROWS = HEAD_V + BF16_SUBLANES
KEY_SLAB_OF_HEAD = (0, 1, 2, 3, 4, 4, 4, 4, 5, 5, 5, 5, 6, 6, 6, 6, 6, 6, 6, 6)
VALUE_HEAD_OF_HEAD = (0, 1, 2, 3, 4, 4, 5, 5, 6, 6, 7, 7, 8, 8, 8, 8, 9, 9, 9, 9)

TM = 256
TQ = 512
TK = 768
KEY_CHUNK = 256
STALE_MAX_SLACK = 64.0

WK_COLS = 1152
WQ_ROWS = 2304


def _rms_rows(x, eps=EPS):
    return lax.rsqrt(jnp.mean(x * x, axis=-1, keepdims=True) + eps)


def _rms_cols(x, eps=EPS):
    return lax.rsqrt(jnp.mean(x * x, axis=0, keepdims=True) + eps)


def _adaln_kernel(c_ref, w_ref, b_ref, o_ref):
    c = c_ref[...]
    sc = c * jax.nn.sigmoid(c)
    o_ref[0] = jnp.dot(sc.astype(BF16), w_ref[0].astype(BF16),
                       preferred_element_type=F32) + b_ref[0]


def _adaln(cvec, w_ada, b_ada):
    depth = w_ada.shape[0]
    tn = 1024
    n_out = w_ada.shape[2]
    return pl.pallas_call(
        _adaln_kernel,
        grid=(depth, n_out // tn),
        in_specs=[
            pl.BlockSpec((8, D_MODEL), lambda l, j: (0, 0)),
            pl.BlockSpec((1, D_MODEL, tn), lambda l, j: (l, 0, j)),
            pl.BlockSpec((1, 1, tn), lambda l, j: (l, 0, j)),
        ],
        out_specs=pl.BlockSpec((1, 8, tn), lambda l, j: (l, 0, j)),
        out_shape=jax.ShapeDtypeStruct((depth, 8, n_out), F32),
        compiler_params=pltpu.CompilerParams(
            dimension_semantics=("arbitrary", "arbitrary"),
            vmem_limit_bytes=VMEM_LIMIT_ADALN),
        name="adaln",
    )(cvec, w_ada, b_ada.reshape(depth, 1, n_out))


def _token_tile(refs, n_parts):
    if n_parts == 1:
        return refs[0][0]
    return jnp.where(pl.program_id(1) == SEQ // TM, refs[1][0], refs[0][0])


def _proj_kernel(*refs, n_x_parts):
    x = _token_tile(refs, n_x_parts)
    (mod_ref, gpre_ref, wk_ref, wqT_ref, wqbT_ref, wkvk_ref, wkvvT_ref,
     gqmla_ref, gkv_tok_ref, gkvT_ref, ggq_ref, ggqrot_ref, ggk_ref, ggkrot_ref,
     cS_ref, sS_ref, cL_ref, sL_ref, cST_ref, sST_ref, cLT_ref, sLT_ref,
     qT_ref, k_ref, vT_ref) = refs[n_x_parts:]
    tm = x.shape[0]
    mod = mod_ref[0]
    sh = mod[:, 0:D_MODEL]
    sc = mod[:, D_MODEL:2 * D_MODEL]
    h = (x * _rms_rows(x) * gpre_ref[...]) * (1.0 + sc) + sh
    hb = h.astype(BF16)
    pk = jnp.dot(hb, wk_ref[...], preferred_element_type=F32)
    pT = lax.dot_general(wqT_ref[...], hb, (((1,), (1,)), ((), ())),
                         preferred_element_type=F32)

    cS = cS_ref[...]
    sS = sS_ref[...]
    cL = cL_ref[...]
    sL = sL_ref[...]
    cST = cST_ref[...]
    sST = sST_ref[...]
    cLT = cLT_ref[...]
    sLT = sLT_ref[...]
    ones_rows = jnp.ones((BF16_SUBLANES, tm), BF16)
    zeros32 = jnp.zeros((32, tm), BF16)
    zeros64 = jnp.zeros((64, tm), BF16)

    qaT = pT[0:256]
    qn = (qaT * _rms_cols(qaT) * gqmla_ref[...]).astype(BF16)
    qm = jnp.dot(wqbT_ref[...], qn, preferred_element_type=F32)
    for hh in range(MLA_HEADS):
        blk = qm[128 * hh:128 * (hh + 1)]
        pe = blk[64:96] * cST + blk[96:128] * sST
        qT_ref[0, hh, 0:64, :] = (blk[0:64] * MLA_QS).astype(BF16)
        qT_ref[0, hh, 64:96, :] = (pe * MLA_QS).astype(BF16)
        qT_ref[0, hh, 96:128, :] = zeros32

    ckv = pk[:, 0:128]
    cn = (ckv * _rms_rows(ckv) * gkv_tok_ref[...]).astype(BF16)
    kn = jnp.dot(cn, wkvk_ref[...], preferred_element_type=F32)
    pe_tok = pk[:, 128:256] * cS[:, 0:128] + pk[:, 256:384] * sS[:, 0:128]
    for hh in range(MLA_HEADS):
        k_ref[0, :, 128 * hh:128 * (hh + 1)] = (kn[:, 128 * hh:128 * (hh + 1)] + pe_tok).astype(BF16)

    ckvT = pT[256:384]
    cnT = (ckvT * _rms_cols(ckvT) * gkvT_ref[...]).astype(BF16)
    vmT = jnp.dot(wkvvT_ref[...], cnT, preferred_element_type=F32)
    for hh in range(MLA_HEADS):
        vT_ref[0, hh, 0:64, :] = vmT[64 * hh:64 * (hh + 1)].astype(BF16)
        vT_ref[0, hh, 64:V_ROWS, :] = ones_rows

    qd = pT[384:640].reshape(8, 32, tm)
    qdr = pT[640:896].reshape(8, 32, tm)
    qd = (qd * cST[None] + qdr * sST[None]) * DIFF_QS
    for j in range(8):
        for rb in range(4):
            val = qd[j].astype(BF16) if rb == j % 4 else zeros32
            qT_ref[0, 4 + j, 32 * rb:32 * (rb + 1), :] = val
    kd = pk[:, 384:640] * cS + pk[:, 640:896] * sS
    k_ref[0, :, 512:768] = kd.astype(BF16)
    vdT = pT[1920:2176]
    for hh in range(DIFF_HEADS):
        vT_ref[0, 4 + hh, 0:64, :] = vdT[64 * hh:64 * (hh + 1)].astype(BF16)
        vT_ref[0, 4 + hh, 64:V_ROWS, :] = ones_rows

    gq = pT[896:1408].reshape(8, 64, tm)
    gqr = pT[1408:1920].reshape(8, 64, tm)
    rq = lax.rsqrt(jnp.mean(gq * gq, axis=1, keepdims=True) + EPS)
    qg = ((gq * rq * ggq_ref[...][None]) * cLT[None]
          + (gqr * rq * ggqrot_ref[...][None]) * sLT[None]) * GQA_QS
    for j in range(8):
        grp = j // 4
        for rb in range(2):
            val = qg[j].astype(BF16) if rb == grp else zeros64
            qT_ref[0, 12 + j, 64 * rb:64 * (rb + 1), :] = val
    gk = pk[:, 896:1024]
    gkr = pk[:, 1024:1152]
    sq = gk * gk
    lane = lax.broadcasted_iota(jnp.int32, sq.shape, 1)
    lo = lane < GQA_DIM
    s0 = jnp.sum(jnp.where(lo, sq, 0.0), axis=-1, keepdims=True)
    s1 = jnp.sum(jnp.where(lo, 0.0, sq), axis=-1, keepdims=True)
    rk = jnp.where(lo, lax.rsqrt(s0 / GQA_DIM + EPS), lax.rsqrt(s1 / GQA_DIM + EPS))
    kg = (gk * rk * ggk_ref[...]) * cL + (gkr * rk * ggkrot_ref[...]) * sL
    k_ref[0, :, 768:896] = kg.astype(BF16)
    gvT = pT[2176:2304]
    for hh in range(GQA_KV_HEADS):
        vT_ref[0, 8 + hh, 0:64, :] = gvT[64 * hh:64 * (hh + 1)].astype(BF16)
        vT_ref[0, 8 + hh, 64:V_ROWS, :] = ones_rows


def _token_specs(x_parts):
    n_lat = SEQ // TM
    if len(x_parts) == 1:
        return [pl.BlockSpec((1, TM, D_MODEL), lambda b, t: (b, t, 0))]
    return [pl.BlockSpec((1, TM, D_MODEL), lambda b, t: (b, jnp.minimum(t, n_lat - 1), 0)),
            pl.BlockSpec((1, CTX_LEN, D_MODEL), lambda b, t: (b, 0, 0))]


def _mod_spec(l):
    n_lat = SEQ // TM
    return pl.BlockSpec((1, 1, 6 * D_MODEL),
                        lambda b, t: (8 * l + jnp.where(t == n_lat, 2, b), 0, 0))


def _layer_spec(l, shape, **kwargs):
    return pl.BlockSpec((None,) + shape, lambda b, t, *_: (l,) + (0,) * len(shape), **kwargs)


def _proj(x_parts, mod3, l, lw, tabs):
    bsz = x_parts[0].shape[0]
    t_all = T_ALL
    nt = t_all // TM
    const = functools.partial(_layer_spec, l)
    in_specs = _token_specs(x_parts) + [
        _mod_spec(l),
        const((1, D_MODEL)),
        const((D_MODEL, WK_COLS)),
        const((WQ_ROWS, D_MODEL)),
        const((512, 256)),
        const((128, 512)),
        const((256, 128)),
        const((256, TM)),
        const((1, 128)),
        const((128, TM)),
        const((64, TM)),
        const((64, TM)),
        const((1, 128)),
        const((1, 128)),
        pl.BlockSpec((TM, 256), lambda b, t: (t, 0)),
        pl.BlockSpec((TM, 256), lambda b, t: (t, 0)),
        pl.BlockSpec((TM, 128), lambda b, t: (t, 0)),
        pl.BlockSpec((TM, 128), lambda b, t: (t, 0)),
        pl.BlockSpec((32, TM), lambda b, t: (0, t)),
        pl.BlockSpec((32, TM), lambda b, t: (0, t)),
        pl.BlockSpec((64, TM), lambda b, t: (0, t)),
        pl.BlockSpec((64, TM), lambda b, t: (0, t)),
    ]
    out_specs = [
        pl.BlockSpec((1, N_SCORE_HEADS, 128, TM), lambda b, t: (b, 0, 0, t)),
        pl.BlockSpec((1, TM, N_KEY_SLABS * 128), lambda b, t: (b, t, 0)),
        pl.BlockSpec((1, N_VALUE_HEADS, V_ROWS, TM), lambda b, t: (b, 0, 0, t)),
    ]
    out_shape = [
        jax.ShapeDtypeStruct((bsz, N_SCORE_HEADS, 128, t_all), BF16),
        jax.ShapeDtypeStruct((bsz, t_all, N_KEY_SLABS * 128), BF16),
        jax.ShapeDtypeStruct((bsz, N_VALUE_HEADS, V_ROWS, t_all), BF16),
    ]
    return pl.pallas_call(
        functools.partial(_proj_kernel, n_x_parts=len(x_parts)),
        grid=(bsz, nt),
        in_specs=in_specs,
        out_specs=out_specs,
        out_shape=out_shape,
        compiler_params=pltpu.CompilerParams(
            dimension_semantics=("parallel", "parallel"),
            vmem_limit_bytes=VMEM_LIMIT_PROJ),
        name="proj",
    )(*x_parts, mod3, lw["g_attn_pre"], lw["wk"], lw["wqT"], lw["wqbT"], lw["wkvk"], lw["wkvvT"],
      lw["gqmla_b"], lw["gkv_tok"], lw["gkvT_b"], lw["ggq_b"], lw["ggqrot_b"], lw["ggk128"],
      lw["ggkrot128"],
      tabs["cS"], tabs["sS"], tabs["cL"], tabs["sL"], tabs["cST"], tabs["sST"], tabs["cLT"],
      tabs["sLT"])


def _attn_kernel(qT_ref, k_ref, vT_ref, lam_ref, gsub_ref, yT_ref, acc_ref, m_ref, kept_ref, *,
                 n_k, lam_init):
    ki = pl.program_id(2)
    tk = k_ref.shape[1]
    n_chunks = tk // KEY_CHUNK

    def scores(h, c):
        slab = KEY_SLAB_OF_HEAD[h]
        ks = k_ref[0, KEY_CHUNK * c:KEY_CHUNK * (c + 1), 128 * slab:128 * (slab + 1)]
        return jnp.dot(ks, qT_ref[0, h], preferred_element_type=F32)

    def weighted_values(h, c, pT):
        v = vT_ref[0, VALUE_HEAD_OF_HEAD[h], :, KEY_CHUNK * c:KEY_CHUNK * (c + 1)]
        return jnp.dot(v, pT, preferred_element_type=F32)

    def column_max(m, s_chunks):
        for s in s_chunks:
            m = jnp.maximum(m, jnp.max(s, axis=0, keepdims=True))
        return m

    def sweep_heads(lookahead, head_step):
        pending = [[scores(h, c) for c in range(n_chunks)] for h in range(lookahead)]
        for h in range(N_SCORE_HEADS):
            s_chunks = pending.pop(0)
            issued = []

            def issue_ahead(c, h=h, issued=issued):
                if h + lookahead < N_SCORE_HEADS:
                    issued.append(scores(h + lookahead, c))

            head_step(h, s_chunks, issue_ahead)
            if issued:
                pending.append(issued)

    @pl.when(ki == 0)
    def _first_tile():
        def head_step(h, s_chunks, issue_ahead):
            m_new = column_max(jnp.full((1, s_chunks[0].shape[1]), -jnp.inf, F32), s_chunks)
            oT = None
            for c, s in enumerate(s_chunks):
                o_c = weighted_values(h, c, jnp.exp2(s - m_new).astype(BF16))
                issue_ahead(c)
                oT = o_c if oT is None else oT + o_c
            acc_ref[h] = oT
            m_ref[h] = m_new

        sweep_heads(2, head_step)

    if n_k > 1:
        @pl.when(ki > 0)
        def _later_tiles():
            excess = []

            def head_step(h, s_chunks, issue_ahead):
                m_used = m_ref[h]
                m_tile = m_used
                oT = None
                for c, s in enumerate(s_chunks):
                    m_tile = jnp.maximum(m_tile, jnp.max(s, axis=0, keepdims=True))
                    o_c = weighted_values(h, c, jnp.exp2(s - m_used).astype(BF16))
                    issue_ahead(c)
                    oT = o_c if oT is None else oT + o_c
                acc_old = acc_ref[h]
                over = m_tile - m_used
                keep_old = over > STALE_MAX_SLACK
                acc_ref[h] = jnp.where(keep_old, acc_old, (acc_old + oT) * jnp.exp2(-over))
                m_ref[h] = jnp.where(keep_old, m_used, m_tile)
                kept_ref[h] = jnp.where(keep_old, 1.0, 0.0)
                excess.append(over)

            sweep_heads(1, head_step)
            worst = excess[0]
            for e in excess[1:]:
                worst = jnp.maximum(worst, e)

            @pl.when(jnp.max(worst) > STALE_MAX_SLACK)
            def _redo_kept_columns():
                def redo_head(h, carry):
                    slab = jnp.where(h < 4, h, jnp.where(h < 12, 4 + ((h - 4) >> 2), 6))
                    vh = jnp.where(h < 4, h, jnp.where(h < 12, 4 + ((h - 4) >> 1),
                                                       8 + ((h - 12) >> 2)))
                    ks = k_ref[0, :, pl.ds(pl.multiple_of(slab * 128, 128), 128)]
                    sT = jnp.dot(ks, qT_ref[0, h], preferred_element_type=F32)
                    m_old = m_ref[h]
                    m_new = jnp.maximum(m_old, jnp.max(sT, axis=0, keepdims=True))
                    kept = kept_ref[h] > 0.5
                    oT = jnp.dot(vT_ref[0, vh], jnp.exp2(sT - m_new).astype(BF16),
                                 preferred_element_type=F32)
                    acc_old = acc_ref[h]
                    acc_ref[h] = jnp.where(kept, acc_old * jnp.exp2(m_old - m_new) + oT, acc_old)
                    m_ref[h] = jnp.where(kept, m_new, m_old)
                    return carry

                lax.fori_loop(0, N_SCORE_HEADS, redo_head, 0)

    @pl.when(ki == n_k - 1)
    def _finalize():
        def head_out(h):
            a = acc_ref[h]
            return a[0:HEAD_V] / a[HEAD_V:HEAD_V + 1]
        for hh in range(MLA_HEADS):
            yT_ref[0, 64 * hh:64 * (hh + 1), :] = head_out(hh).astype(BF16)
        lp = lam_ref[...]
        l1 = jnp.sum(lp[0:1] * lp[1:2], axis=-1, keepdims=True)
        l2 = jnp.sum(lp[2:3] * lp[3:4], axis=-1, keepdims=True)
        lam = jnp.exp(l1) - jnp.exp(l2) + lam_init
        gsub = gsub_ref[...]
        for hh in range(DIFF_HEADS):
            d = head_out(4 + 2 * hh) - lam * head_out(5 + 2 * hh)
            y = (d * _rms_cols(d) * gsub) * (1.0 - lam_init)
            yT_ref[0, 256 + 64 * hh:256 + 64 * (hh + 1), :] = y.astype(BF16)
        for j in range(GQA_HEADS):
            yT_ref[0, 512 + 64 * j:512 + 64 * (j + 1), :] = head_out(12 + j).astype(BF16)


def _attn(qT, kslab, vT, lam_vecs, gsub_b, l, lam_init, *, tq, tk, q_blocks, q_off, k_blocks,
          k_off, name):
    bsz = qT.shape[0]
    return pl.pallas_call(
        functools.partial(_attn_kernel, n_k=k_blocks, lam_init=lam_init),
        grid=(bsz, q_blocks, k_blocks),
        in_specs=[
            pl.BlockSpec((1, N_SCORE_HEADS, 128, tq), lambda b, qi, ki: (b, 0, 0, qi + q_off)),
            pl.BlockSpec((1, tk, N_KEY_SLABS * 128), lambda b, qi, ki: (b, ki + k_off, 0)),
            pl.BlockSpec((1, N_VALUE_HEADS, V_ROWS, tk), lambda b, qi, ki: (b, 0, 0, ki + k_off)),
            _layer_spec(l, (4, DIFF_QK)),
            _layer_spec(l, (HEAD_V, tq)),
        ],
        out_specs=pl.BlockSpec((1, MIX_WIDTH, tq), lambda b, qi, ki: (b, 0, qi)),
        out_shape=jax.ShapeDtypeStruct((bsz, MIX_WIDTH, q_blocks * tq), BF16),
        scratch_shapes=[
            pltpu.VMEM((N_SCORE_HEADS, V_ROWS, tq), F32),
            pltpu.VMEM((N_SCORE_HEADS, 1, tq), F32),
            pltpu.VMEM((N_SCORE_HEADS, 1, tq), F32),
        ],
        compiler_params=pltpu.CompilerParams(
            dimension_semantics=("parallel", "parallel", "arbitrary"),
            vmem_limit_bytes=VMEM_LIMIT_ATTN),
        name=name,
    )(qT, kslab, vT, lam_vecs, gsub_b)


def _out_ffn_kernel(*refs, n_lat_tiles, has_ctx, n_x_parts):
    if has_ctx:
        yT_ref, yTc_ref = refs[:2]
        refs = refs[2:]
        yT = jnp.where(pl.program_id(1) == n_lat_tiles, yTc_ref[0], yT_ref[0])
    else:
        yT = refs[0][0]
        refs = refs[1:]
    x = _token_tile(refs, n_x_parts)
    mod_ref, gpost_ref, gfpre_ref, gfpost_ref, wout_ref, wg_ref, wu_ref, wd_ref, o_ref = \
        refs[n_x_parts:]
    yp = lax.dot_general(yT, wout_ref[...], (((0,), (0,)), ((), ())),
                         preferred_element_type=F32)
    mod = mod_ref[0]
    gt_a = mod[:, 2 * D_MODEL:3 * D_MODEL]
    sh_f = mod[:, 3 * D_MODEL:4 * D_MODEL]
    sc_f = mod[:, 4 * D_MODEL:5 * D_MODEL]
    gt_f = mod[:, 5 * D_MODEL:6 * D_MODEL]
    x1 = x + gt_a * (yp * _rms_rows(yp) * gpost_ref[...])
    hf = (x1 * _rms_rows(x1) * gfpre_ref[...]) * (1.0 + sc_f) + sh_f
    hb = hf.astype(BF16)
    g = jnp.dot(hb, wg_ref[...], preferred_element_type=F32)
    u = jnp.dot(hb, wu_ref[...], preferred_element_type=F32)
    a = (g * jax.nn.sigmoid(g)) * u
    f = jnp.dot(a.astype(BF16), wd_ref[...], preferred_element_type=F32)
    o_ref[0] = x1 + gt_f * (f * _rms_rows(f) * gfpost_ref[...])


def _out_ffn(yT, yT_ctx, x_parts, mod3, l, lw):
    bsz = x_parts[0].shape[0]
    n_lat = SEQ // TM
    has_ctx = yT_ctx is not None
    n_tiles = n_lat + 1 if has_ctx else n_lat
    const = functools.partial(_layer_spec, l, pipeline_mode=pl.Buffered(1))
    y_specs = [pl.BlockSpec((1, MIX_WIDTH, TM), lambda b, t: (b, 0, jnp.minimum(t, n_lat - 1)))]
    y_args = [yT]
    if has_ctx:
        y_specs.append(pl.BlockSpec((1, MIX_WIDTH, TM), lambda b, t: (b, 0, 0)))
        y_args.append(yT_ctx)
    return pl.pallas_call(
        functools.partial(_out_ffn_kernel, n_lat_tiles=n_lat, has_ctx=has_ctx,
                          n_x_parts=len(x_parts)),
        grid=(bsz, n_tiles),
        in_specs=y_specs + _token_specs(x_parts) + [
            _mod_spec(l),
            const((1, D_MODEL)),
            const((1, D_MODEL)),
            const((1, D_MODEL)),
            const((MIX_WIDTH, D_MODEL)),
            const((D_MODEL, FFN_HIDDEN)),
            const((D_MODEL, FFN_HIDDEN)),
            const((FFN_HIDDEN, D_MODEL)),
        ],
        out_specs=pl.BlockSpec((1, TM, D_MODEL), lambda b, t: (b, t, 0)),
        out_shape=jax.ShapeDtypeStruct((bsz, n_tiles * TM, D_MODEL), F32),
        compiler_params=pltpu.CompilerParams(
            dimension_semantics=("parallel", "parallel"),
            vmem_limit_bytes=VMEM_LIMIT_OUT_FFN),
        name="out_ffn",
    )(*y_args, *x_parts, mod3, lw["g_attn_post"], lw["g_ffn_pre"], lw["g_ffn_post"],
      lw["w_out"], lw["w_gate"], lw["w_up"], lw["w_down"])


def _rot_cols(w, half):
    n = w.shape[-1]
    wg = w.reshape(w.shape[:-1] + (n // (2 * half), 2, half))
    return jnp.concatenate([-wg[..., 1:2, :], wg[..., 0:1, :]], axis=-2).reshape(w.shape)


def _swap_halves(g, half):
    return jnp.concatenate([g[..., half:], g[..., :half]], axis=-1)


def _rope_tables():
    t = jnp.arange(SEQ, dtype=jnp.int32)
    row = (t // GRID_W).astype(F32)
    col = (t % GRID_W).astype(F32)

    def tables(rot_dim):
        quarter = rot_dim // 4
        inv = ROPE_THETA ** (-jnp.arange(quarter, dtype=F32) / quarter)
        ang = jnp.concatenate([row[:, None] * inv, col[:, None] * inv], axis=-1)
        cos = jnp.concatenate([jnp.cos(ang), jnp.ones((CTX_LEN, rot_dim // 2), F32)], axis=0)
        sin = jnp.concatenate([jnp.sin(ang), jnp.zeros((CTX_LEN, rot_dim // 2), F32)], axis=0)
        return jnp.tile(cos, (1, 2)), jnp.tile(sin, (1, 2))

    c_s, s_s = tables(MLA_ROPE)
    c_l, s_l = tables(GQA_DIM)
    return {
        "cS": jnp.tile(c_s, (1, 8)), "sS": jnp.tile(s_s, (1, 8)),
        "cL": jnp.tile(c_l, (1, 2)), "sL": jnp.tile(s_l, (1, 2)),
        "cST": c_s.T, "sST": s_s.T,
        "cLT": c_l.T, "sLT": s_l.T,
    }


def _stacked_weights(w_in, g_attn_pre, g_attn_post, g_mla_q, w_mla_qb, g_mla_kv, w_mla_kvb,
                     g_diff_sub, g_gqa_q, g_gqa_k, w_out, g_ffn_pre, g_ffn_post, w_ffn_gate,
                     w_ffn_up, w_ffn_down, lambdas):
    depth = w_in.shape[0]
    q_a = w_in[..., 0:256]
    c_kv = w_in[..., 256:384]
    k_pe = w_in[..., 384:416]
    dq = w_in[..., 416:672]
    dk = w_in[..., 672:928]
    dv = w_in[..., 928:1184]
    gq = w_in[..., 1184:1696]
    gk = w_in[..., 1696:1824]
    gv = w_in[..., 1824:1952]
    zeros = lambda n: jnp.zeros((depth, D_MODEL, n), F32)
    pe_slab = jnp.concatenate([zeros(64), k_pe, zeros(32)], axis=-1)
    pe_rot_slab = jnp.concatenate([zeros(64), _rot_cols(k_pe, 16), zeros(32)], axis=-1)
    wk = jnp.concatenate([c_kv, pe_slab, pe_rot_slab, dk, _rot_cols(dk, 16), gk, _rot_cols(gk, 32)],
                         axis=-1)
    wq = jnp.concatenate([q_a, c_kv, dq, _rot_cols(dq, 16), gq, _rot_cols(gq, 32), dv, gv], axis=-1)
    qb = w_mla_qb.reshape(depth, MLA_Q_RANK, MLA_HEADS, MLA_NOPE + MLA_ROPE)
    wqb = jnp.concatenate([qb, _rot_cols(qb[..., MLA_NOPE:], 16)], axis=-1)
    wqb = wqb.reshape(depth, MLA_Q_RANK, 512)
    kvb = w_mla_kvb.reshape(depth, MLA_KV_RANK, MLA_HEADS, MLA_NOPE + MLA_V)
    wkvk = jnp.concatenate([kvb[..., :MLA_NOPE],
                            jnp.zeros((depth, MLA_KV_RANK, MLA_HEADS, 64), F32)], axis=-1)
    wkvk = wkvk.reshape(depth, MLA_KV_RANK, 512)
    wkvv = kvb[..., MLA_NOPE:].reshape(depth, MLA_KV_RANK, MLA_HEADS * MLA_V)
    transposed = lambda w: jnp.swapaxes(w, 1, 2).astype(BF16)
    row = lambda g: g[:, None, :]
    column = lambda g, n: jnp.broadcast_to(g[:, :, None], g.shape + (n,))
    return {
        "g_attn_pre": row(g_attn_pre), "g_attn_post": row(g_attn_post),
        "g_ffn_pre": row(g_ffn_pre), "g_ffn_post": row(g_ffn_post),
        "wk": wk.astype(BF16), "wqT": transposed(wq), "wqbT": transposed(wqb),
        "wkvk": wkvk.astype(BF16), "wkvvT": transposed(wkvv),
        "gqmla_b": column(g_mla_q, TM), "gkv_tok": row(g_mla_kv), "gkvT_b": column(g_mla_kv, TM),
        "ggq_b": column(g_gqa_q, TM), "ggqrot_b": column(_swap_halves(g_gqa_q, 32), TM),
        "ggk128": row(jnp.tile(g_gqa_k, (1, 2))),
        "ggkrot128": row(jnp.tile(_swap_halves(g_gqa_k, 32), (1, 2))),
        "gsub_b": column(g_diff_sub, TQ),
        "lam_vecs": jnp.stack(lambdas, axis=1),
        "w_out": w_out.astype(BF16), "w_gate": w_ffn_gate.astype(BF16),
        "w_up": w_ffn_up.astype(BF16), "w_down": w_ffn_down.astype(BF16),
    }


def kernel(x, c, ctx, c_ctx, w_ada, b_ada, g_attn_pre, g_attn_post, w_in, g_mla_q, w_mla_qb, g_mla_kv, w_mla_kvb, lambda_q1, lambda_k1, lambda_q2, lambda_k2, g_diff_sub, g_gqa_q, g_gqa_k, w_out, g_ffn_pre, g_ffn_post, w_ffn_gate, w_ffn_up, w_ffn_down):
    bsz = x.shape[0]
    depth = w_ada.shape[0]
    assert x.shape == (bsz, SEQ, D_MODEL) and ctx.shape == (bsz, CTX_LEN, D_MODEL) and bsz == 2
    tabs = _rope_tables()
    cvec = jnp.concatenate([c, c_ctx[None], jnp.zeros((8 - bsz - 1, D_MODEL), F32)], axis=0)
    mod3 = _adaln(cvec, w_ada, b_ada).reshape(depth * 8, 1, 6 * D_MODEL)
    lw = _stacked_weights(w_in, g_attn_pre, g_attn_post, g_mla_q, w_mla_qb, g_mla_kv, w_mla_kvb,
                          g_diff_sub, g_gqa_q, g_gqa_k, w_out, g_ffn_pre, g_ffn_post, w_ffn_gate,
                          w_ffn_up, w_ffn_down, (lambda_q1, lambda_k1, lambda_q2, lambda_k2))
    x_parts = (x, ctx)
    for l in range(depth):
        last = l == depth - 1
        lam_init = 0.8 - 0.6 * math.exp(-0.3 * l)
        qT, kslab, vT = _proj(x_parts, mod3, l, lw, tabs)
        yT = _attn(qT, kslab, vT, lw["lam_vecs"], lw["gsub_b"], l, lam_init, tq=TQ, tk=TK,
                   q_blocks=SEQ // TQ, q_off=0, k_blocks=T_ALL // TK, k_off=0, name="attn")
        yT_ctx = None
        if not last:
            yT_ctx = _attn(qT, kslab, vT, lw["lam_vecs"], lw["gsub_b"], l, lam_init,
                           tq=CTX_LEN, tk=CTX_LEN, q_blocks=1, q_off=SEQ // CTX_LEN, k_blocks=1,
                           k_off=SEQ // CTX_LEN, name="attn_ctx")
        x_parts = (_out_ffn(yT, yT_ctx, x_parts, mod3, l, lw),)
    return x_parts[0]
```
